```python
import math
import jax
import jax.numpy as jnp
from jax import lax
import numpy as np

D_MODEL = 1024
BATCH = 8
SEQ = 8192
DEPTH = 1
DEC_BATCH = 8
DEC_SEQ = 4096
PAST_LEN = 128

EPS = 1e-6
HEAD_DIM = D_MODEL // 16
N_HEADS = 12
DILATED_GROUPS = ((128, 1), (512, 4), (2048, 16))
HEADS_PER_GROUP = N_HEADS // len(DILATED_GROUPS)
ATTN_WIDTH = N_HEADS * HEAD_DIM
ATTN_OUT_WIDTH = HEADS_PER_GROUP * HEAD_DIM
NUM_BUCKETS = 32
MAX_DISTANCE = 1024
NEG_INF = -1e30
D_HYENA = 3 * D_MODEL // 4
SHORT_CONV = 3
FILTER_BANDS = 16
FILTER_EMB = 1 + 2 * FILTER_BANDS
FILTER_HIDDEN = 64
DECAY_TARGET = 1e-2
FAST_DECAY_PCT = 0.3
SLOW_DECAY_PCT = 1.5
N_BRANCHES = 2
D_FF = 4 * D_MODEL
IN_PROJ_WIDTH = 3 * D_HYENA + 3 * ATTN_WIDTH + N_BRANCHES * D_MODEL

kernel_name = 'hyena_dilated_attn_encoder'


def rmsnorm(x, g):
    xf = x.astype(jnp.float32)
    y = xf * lax.rsqrt(jnp.mean(xf * xf, axis=-1, keepdims=True) + EPS)
    return (y * g.astype(jnp.float32)).astype(x.dtype)


def t5_bucket(rel):
    half = NUM_BUCKETS // 2
    max_exact = half // 2
    n = jnp.abs(rel)
    ret = jnp.where(rel > 0, half, 0)
    large = max_exact + (jnp.log(jnp.maximum(n, 1).astype(jnp.float32) / max_exact)
                         / math.log(MAX_DISTANCE / max_exact) * (half - max_exact)).astype(jnp.int32)
    large = jnp.minimum(large, half - 1)
    return ret + jnp.where(n < max_exact, n, large)


def banded_attention(q, k, v, bias_vec, radius):
    blk = radius
    S, hd = q.shape[-2], q.shape[-1]
    nb = -(-S // blk)
    sp = nb * blk
    lead = q.shape[:-2]
    def pad_cfg(lo, hi):
        return [(0, 0)] * len(lead) + [(lo, hi), (0, 0)]
    qb = jnp.pad(q, pad_cfg(0, sp - S)).reshape(lead + (nb, blk, hd))
    def windows(t):
        tb = jnp.pad(t, pad_cfg(radius, sp - S + radius)).reshape(lead + (nb + 2, blk, hd))
        return jnp.concatenate([tb[..., i:i + nb, :, :] for i in range(3)], axis=-2)
    kw, vw = windows(k), windows(v)
    s = jnp.einsum('...bqd,...bkd->...bqk', qb, kw, preferred_element_type=jnp.float32)
    qi = jnp.arange(blk)[:, None]
    kj = jnp.arange(3 * blk)[None, :]
    rel = kj - radius - qi
    key_pos = jnp.arange(nb)[:, None, None] * blk + (kj - radius)[None]
    valid = (jnp.abs(rel) <= radius)[None] & (key_pos >= 0) & (key_pos < S)
    bias = bias_vec.astype(jnp.float32)[:, jnp.clip(rel + radius, 0, 2 * radius)][:, None]
    s = jnp.where(valid, s + bias, NEG_INF)
    m = jnp.max(s, axis=-1, keepdims=True)
    p = jnp.exp(s - m)
    den = jnp.sum(p, axis=-1, keepdims=True)
    o = jnp.einsum('...bqk,...bkd->...bqd', p, vw.astype(jnp.float32)) / den
    lse = (m + jnp.log(den))[..., 0]
    o = o.reshape(lead + (sp, hd))[..., :S, :]
    lse = lse.reshape(lead + (sp,))[..., :S]
    return o, lse


def dilated_group(q, k, v, bias_tab, window, dil):
    B, L, Hg, hd = q.shape
    S = L // dil
    radius = window // (2 * dil)
    def to_sub(t):
        return t.reshape(B, S, dil, Hg, hd).transpose(0, 2, 3, 1, 4)
    offs = jnp.arange(-radius, radius + 1) * dil
    bias_vec = bias_tab[t5_bucket(offs)].T
    o, lse = banded_attention(to_sub(q), to_sub(k), to_sub(v), bias_vec, radius)
    o = o.transpose(0, 3, 1, 2, 4).reshape(B, L, Hg, hd)
    lse = lse.transpose(0, 3, 1, 2).reshape(B, L, Hg)
    return o, lse


def dilated_attention(q, k, v, rel_bias, q_norm_g, k_norm_g):
    B, L, _ = q.shape
    q = rmsnorm(q.reshape(B, L, N_HEADS, HEAD_DIM), q_norm_g) * (HEAD_DIM ** -0.5)
    k = rmsnorm(k.reshape(B, L, N_HEADS, HEAD_DIM), k_norm_g)
    v = v.reshape(B, L, N_HEADS, HEAD_DIM)
    outs, lses = [], []
    for gi, (window, dil) in enumerate(DILATED_GROUPS):
        hs = slice(gi * HEADS_PER_GROUP, (gi + 1) * HEADS_PER_GROUP)
        o, lse = dilated_group(q[:, :, hs], k[:, :, hs], v[:, :, hs], rel_bias[:, hs], window, dil)
        outs.append(o)
        lses.append(lse)
    alpha = jax.nn.softmax(jnp.stack(lses), axis=0)
    o = jnp.sum(alpha[..., None] * jnp.stack(outs), axis=0)
    return o.reshape(B, L, ATTN_OUT_WIDTH).astype(q.dtype)


def short_conv(z, w, b):
    zp = jnp.pad(z, ((0, 0), (1, 1), (0, 0)))
    return zp[:, :-2] * w[0] + zp[:, 1:-1] * w[1] + zp[:, 2:] * w[2] + b


def implicit_filter(L, w1, b1, w2, b2, w3, b3, freq, w_out):
    f32 = jnp.float32
    t = jnp.linspace(0.0, 1.0, L, dtype=f32)[:, None]
    bands = jnp.linspace(1e-4, FILTER_BANDS - 1, FILTER_BANDS, dtype=f32)[None, :]
    w = 2.0 * math.pi * jnp.arange(L, dtype=f32)[:, None] / L
    feats = jnp.concatenate([t, jnp.cos(bands * w), -jnp.sin(bands * w)], axis=-1)
    fr = freq.astype(f32)
    h = jnp.sin(fr * (feats @ w1.astype(f32) + b1.astype(f32)))
    h = jnp.sin(fr * (h @ w2.astype(f32) + b2.astype(f32)))
    h = jnp.sin(fr * (h @ w3.astype(f32) + b3.astype(f32)))
    h = h @ w_out.astype(f32)
    deltas = jnp.abs(jnp.linspace(math.log(DECAY_TARGET) / SLOW_DECAY_PCT,
                                  math.log(DECAY_TARGET) / FAST_DECAY_PCT, D_HYENA, dtype=f32))
    decay = jnp.exp(-t * deltas[None, :])
    h_fwd, h_bwd = jnp.split(h, 2, axis=-1)
    h_fwd, h_bwd = h_fwd * decay, h_bwd * decay
    k2 = jnp.concatenate([h_fwd, jnp.zeros((1, D_HYENA), f32), h_bwd[1:][::-1]], axis=0)
    return k2 / jnp.sum(jnp.abs(k2), axis=0, keepdims=True)


def long_conv(u, k2, d):
    L = u.shape[1]
    uf = u.astype(jnp.float32)
    U = jnp.fft.rfft(uf, n=2 * L, axis=1)
    K = jnp.fft.rfft(k2, n=2 * L, axis=0)
    y = jnp.fft.irfft(U * K[None], n=2 * L, axis=1)[:, :L]
    return (y + uf * d.astype(jnp.float32)).astype(u.dtype)


def hyena_branch(z, conv_w, conv_b, w1, b1, w2, b2, w3, b3, freq, w_out, hyena_d):
    L = z.shape[1]
    zc = short_conv(z, conv_w, conv_b)
    x0, x1, v = jnp.split(zc, 3, axis=-1)
    k2 = implicit_filter(L, w1, b1, w2, b2, w3, b3, freq, w_out)
    return x0 * long_conv(x1 * v, k2, hyena_d)


def encoder_layer(x, c, rel_bias, ada_w, ada_b, norm1_g, w_in, conv_w, conv_b,
                  filt_w1, filt_b1, filt_w2, filt_b2, filt_w3, filt_b3, filt_freq, filt_w_out,
                  hyena_d, q_norm_g, k_norm_g, w_hy_br, w_at_br, w_out, norm2_g, w_up, w_down):
    mod = jax.nn.silu(c) @ ada_w + ada_b
    sh1, sc1, gt1, sh2, sc2, gt2 = jnp.split(mod[:, None, :], 6, axis=-1)
    u = rmsnorm(x, norm1_g) * (1.0 + sc1) + sh1
    z = u @ w_in
    o1 = 3 * D_HYENA
    z_hy, q, k, v, g = jnp.split(z, [o1, o1 + ATTN_WIDTH, o1 + 2 * ATTN_WIDTH, o1 + 3 * ATTN_WIDTH], axis=-1)
    y_hy = hyena_branch(z_hy, conv_w, conv_b, filt_w1, filt_b1, filt_w2, filt_b2,
                        filt_w3, filt_b3, filt_freq, filt_w_out, hyena_d)
    y_at = dilated_attention(q, k, v, rel_bias, q_norm_g, k_norm_g)
    g_hy, g_at = jnp.split(jax.nn.sigmoid(g), 2, axis=-1)
    mixed = (g_hy * (y_hy @ w_hy_br) + g_at * (y_at @ w_at_br)) @ w_out
    h = x + gt1 * mixed
    u2 = rmsnorm(h, norm2_g) * (1.0 + sc2) + sh2
    ff = jnp.square(jax.nn.relu(u2 @ w_up)) @ w_down
    return h + gt2 * ff


def setup_inputs(seed: int = 0) -> dict:
    key = jax.random.key(seed)
    ks = jax.random.split(key, 32)
    def nrm(k, shape, scale):
        return jax.random.normal(k, shape, jnp.float32) * scale
    return {
        'x_prompt': nrm(ks[0], (BATCH, SEQ, D_MODEL), 1.0),
        'x_sample': nrm(ks[1], (DEC_BATCH, DEC_SEQ, D_MODEL), 1.0),
        'c_prompt': nrm(ks[2], (BATCH, D_MODEL), 1.0),
        'c_sample': nrm(ks[3], (DEC_BATCH, D_MODEL), 1.0),
        'rel_bias': nrm(ks[4], (NUM_BUCKETS, N_HEADS), 0.5),
        'ada_w': nrm(ks[5], (DEPTH, D_MODEL, 6 * D_MODEL), D_MODEL ** -0.5),
        'ada_b': nrm(ks[6], (DEPTH, 6 * D_MODEL), 0.02),
        'norm1_g': 1.0 + nrm(ks[7], (DEPTH, D_MODEL), 0.02),
        'w_in': nrm(ks[8], (DEPTH, D_MODEL, IN_PROJ_WIDTH), D_MODEL ** -0.5),
        'conv_w': nrm(ks[9], (DEPTH, SHORT_CONV, 3 * D_HYENA), SHORT_CONV ** -0.5),
        'conv_b': nrm(ks[10], (DEPTH, 3 * D_HYENA), 0.02),
        'filt_w1': nrm(ks[11], (DEPTH, FILTER_EMB, FILTER_HIDDEN), FILTER_EMB ** -0.5),
        'filt_b1': nrm(ks[12], (DEPTH, FILTER_HIDDEN), 0.02),
        'filt_w2': nrm(ks[13], (DEPTH, FILTER_HIDDEN, FILTER_HIDDEN), FILTER_HIDDEN ** -0.5),
        'filt_b2': nrm(ks[14], (DEPTH, FILTER_HIDDEN), 0.02),
        'filt_w3': nrm(ks[15], (DEPTH, FILTER_HIDDEN, FILTER_HIDDEN), FILTER_HIDDEN ** -0.5),
        'filt_b3': nrm(ks[16], (DEPTH, FILTER_HIDDEN), 0.02),
        'filt_freq': 1.0 + nrm(ks[17], (DEPTH, FILTER_HIDDEN), 0.02),
        'filt_w_out': nrm(ks[18], (DEPTH, FILTER_HIDDEN, 2 * D_HYENA), FILTER_HIDDEN ** -0.5),
        'hyena_d': nrm(ks[19], (DEPTH, D_HYENA), 0.1),
        'q_norm_g': 1.0 + nrm(ks[20], (DEPTH, N_HEADS, HEAD_DIM), 0.02),
        'k_norm_g': 1.0 + nrm(ks[21], (DEPTH, N_HEADS, HEAD_DIM), 0.02),
        'w_hy_br': nrm(ks[22], (DEPTH, D_HYENA, D_MODEL), D_HYENA ** -0.5),
        'w_at_br': nrm(ks[23], (DEPTH, ATTN_OUT_WIDTH, D_MODEL), ATTN_OUT_WIDTH ** -0.5),
        'w_out': nrm(ks[24], (DEPTH, D_MODEL, D_MODEL), D_MODEL ** -0.5),
        'norm2_g': 1.0 + nrm(ks[25], (DEPTH, D_MODEL), 0.02),
        'w_up': nrm(ks[26], (DEPTH, D_MODEL, D_FF), D_MODEL ** -0.5),
        'w_down': nrm(ks[27], (DEPTH, D_FF, D_MODEL), D_FF ** -0.5),
    }


def reference(x_prompt, x_sample, c_prompt, c_sample, rel_bias, ada_w, ada_b, norm1_g, w_in,
              conv_w, conv_b, filt_w1, filt_b1, filt_w2, filt_b2, filt_w3, filt_b3, filt_freq,
              filt_w_out, hyena_d, q_norm_g, k_norm_g, w_hy_br, w_at_br, w_out, norm2_g, w_up, w_down):
    y_prompt, y_sample = x_prompt, x_sample
    for l in range(DEPTH):
        layer_params = (ada_w[l], ada_b[l], norm1_g[l], w_in[l], conv_w[l], conv_b[l],
                        filt_w1[l], filt_b1[l], filt_w2[l], filt_b2[l], filt_w3[l], filt_b3[l],
                        filt_freq[l], filt_w_out[l], hyena_d[l], q_norm_g[l], k_norm_g[l],
                        w_hy_br[l], w_at_br[l], w_out[l], norm2_g[l], w_up[l], w_down[l])
        y_prompt = encoder_layer(y_prompt, c_prompt, rel_bias, *layer_params)
        y_sample = encoder_layer(y_sample, c_sample, rel_bias, *layer_params)
    return (y_prompt, y_sample)
```

```python
import functools
import math

import numpy as np
import jax
import jax.numpy as jnp
from jax import lax
from jax.experimental import pallas as pl
from jax.experimental.pallas import tpu as pltpu

F32 = jnp.float32
BF16 = jnp.bfloat16
HIGHEST = lax.Precision.HIGHEST

D_MODEL = 1024
EPS = 1e-6
HEAD_DIM = 64
N_HEADS = 12
DILATED_GROUPS = ((128, 1), (512, 4), (2048, 16))
HEADS_PER_GROUP = 4
ATTN_WIDTH = 768
GROUP_WIDTH = HEADS_PER_GROUP * HEAD_DIM
RADIUS = 64
NUM_BUCKETS = 32
MAX_DISTANCE = 1024
NEG_INF = -1e30
D_HYENA = 768
FILTER_BANDS = 16
FILTER_HIDDEN = 64
DECAY_TARGET = 1e-2
FAST_DECAY_PCT = 0.3
SLOW_DECAY_PCT = 1.5
D_FF = 4096
IN_PROJ_WIDTH = 6656

DFT_N1 = 128
DFT_H1 = 64
DFT_KP = 65
QBLK = 128
VMEM_LIMIT = 56 * 1024 * 1024


def _cparams(*sem):
    return pltpu.CompilerParams(dimension_semantics=sem, vmem_limit_bytes=VMEM_LIMIT)


def _const_spec(shape):
    nd = len(shape)
    return pl.BlockSpec(shape, lambda *_: (0,) * nd, pipeline_mode=pl.Buffered(1))


def _dot(a, b, precision=None):
    return jnp.dot(a, b, preferred_element_type=F32, precision=precision)


@functools.lru_cache(maxsize=None)
def _dft_consts(L):
    n2n = L // DFT_H1
    n = 2 * L
    n1 = np.arange(DFT_N1)[None, None, :]
    k1 = np.arange(DFT_KP)[None, :, None]
    n2 = np.arange(n2n)[:, None, None]
    ang = -2.0 * np.pi * (n2 * k1 / n + k1 * n1 / DFT_N1)
    f1 = np.concatenate([np.cos(ang), np.sin(ang)], axis=1)
    a2 = -2.0 * np.pi * np.arange(n2n)[:, None] * np.arange(n2n)[None, :] / n2n
    fr, fi = np.cos(a2), np.sin(a2)
    g2 = np.block([[fr, -fi], [fi, fr]])
    g2i = np.block([[fr, fi], [-fi, fr]])
    n1o = np.arange(DFT_H1)[None, :, None]
    k1c = np.arange(DFT_KP)[None, None, :]
    angc = 2.0 * np.pi * (n1o * k1c / DFT_N1 + n2 * k1c / n)
    mult = np.full((1, 1, DFT_KP), 2.0)
    mult[0, 0, 0] = 1.0
    mult[0, 0, DFT_KP - 1] = 1.0
    minv = np.concatenate([mult * np.cos(angc), -mult * np.sin(angc)], axis=2) / n
    as_bf16 = lambda a: np.asarray(a, np.float32).astype(BF16)
    return dict(f1_full=as_bf16(f1), f1_half=as_bf16(f1[:, :, :DFT_H1]), g2=as_bf16(g2),
                g2i=as_bf16(g2i), minv=as_bf16(minv))


def _t5_bucket_np(rel):
    half = NUM_BUCKETS // 2
    max_exact = half // 2
    n = np.abs(rel)
    ret = np.where(rel > 0, half, 0)
    large = max_exact + (np.log(np.maximum(n, 1).astype(np.float32) / np.float32(max_exact))
                         / np.float32(math.log(MAX_DISTANCE / max_exact))
                         * np.float32(half - max_exact)).astype(np.int32)
    large = np.minimum(large, half - 1)
    return ret + np.where(n < max_exact, n, large)


@functools.lru_cache(maxsize=None)
def _bucket_tables():
    qi = np.arange(QBLK)[:, None]
    c = np.arange(2 * QBLK)[None, :]
    rel = c - RADIUS - qi
    tabs = []
    for _, dil in DILATED_GROUPS:
        b = _t5_bucket_np(rel * dil)
        tabs.append(np.where(np.abs(rel) <= RADIUS, b, -1))
    return np.stack(tabs).astype(np.int32)


@functools.lru_cache(maxsize=None)
def _small_consts():
    bands = np.linspace(1e-4, FILTER_BANDS - 1, FILTER_BANDS, dtype=np.float32)[None, :]
    deltas = np.abs(np.linspace(math.log(DECAY_TARGET) / SLOW_DECAY_PCT,
                                math.log(DECAY_TARGET) / FAST_DECAY_PCT, D_HYENA, dtype=np.float32))[None, :]
    head = np.arange(ATTN_WIDTH) // HEAD_DIM
    blockdiag = (head[:, None] == head[None, :]).astype(np.float32) / HEAD_DIM
    return bands, deltas, blockdiag.astype(BF16)


def _mod_kernel(c_ref, w_ref, b_ref, o_ref):
    c = c_ref[...]
    s = c / (1.0 + jnp.exp(-c))
    o_ref[...] = _dot(s, w_ref[...], HIGHEST) + b_ref[...]


def _modulation(c, ada_w, ada_b):
    nb, _ = c.shape
    nw = ada_w.shape[1]
    tn = 1024
    return pl.pallas_call(
        _mod_kernel,
        name="mod",
        grid=(nw // tn,),
        in_specs=[pl.BlockSpec((nb, D_MODEL), lambda j: (0, 0)),
                  pl.BlockSpec((D_MODEL, tn), lambda j: (0, j)),
                  pl.BlockSpec((1, tn), lambda j: (0, j))],
        out_specs=pl.BlockSpec((nb, tn), lambda j: (0, j)),
        out_shape=jax.ShapeDtypeStruct((nb, nw), F32),
        compiler_params=_cparams("parallel"),
    )(c, ada_w, ada_b.reshape(1, nw))


def _bias_kernel(rb_ref, bk_ref, o_ref):
    g = pl.program_id(0)
    bk = bk_ref[0]
    for h in range(HEADS_PER_GROUP):
        acc = jnp.full(bk.shape, NEG_INF, F32)
        for b in range(NUM_BUCKETS):
            acc = jnp.where(bk == b, rb_ref[b, g * HEADS_PER_GROUP + h], acc)
        o_ref[0, h] = acc


def _bias_tables(rel_bias):
    bk = jnp.asarray(_bucket_tables())
    ng = len(DILATED_GROUPS)
    return pl.pallas_call(
        _bias_kernel,
        name="bias_tab",
        grid=(ng,),
        in_specs=[pl.BlockSpec(memory_space=pltpu.SMEM),
                  pl.BlockSpec((1, QBLK, 2 * QBLK), lambda g: (g, 0, 0))],
        out_specs=pl.BlockSpec((1, HEADS_PER_GROUP, QBLK, 2 * QBLK), lambda g: (g, 0, 0, 0)),
        out_shape=jax.ShapeDtypeStruct((ng, HEADS_PER_GROUP, QBLK, 2 * QBLK), F32),
        compiler_params=_cparams("arbitrary"),
    )(rel_bias, bk)


def _filter_kernel(w1t_ref, w1c_ref, w1s_ref, b1_ref, w2_ref, b2_ref, w3_ref, b3_ref, fr_ref, wo_ref,
                   bands_ref, deltas_ref, k_ref, asum_ref, *, seq, rows):
    i = pl.program_id(0)
    m = i * rows + lax.broadcasted_iota(jnp.int32, (rows, 1), 0)
    pos = jnp.where(m < seq, m, 2 * seq - m).astype(F32)
    t = pos / float(seq - 1)
    w = (2.0 * math.pi) * pos / float(seq)
    arg = w * bands_ref[...]
    fr = fr_ref[...]
    z = t * w1t_ref[...] + _dot(jnp.cos(arg), w1c_ref[...], HIGHEST) \
        + _dot(-jnp.sin(arg), w1s_ref[...], HIGHEST) + b1_ref[...]
    h = jnp.sin(fr * z)
    h = jnp.sin(fr * (_dot(h, w2_ref[...], HIGHEST) + b2_ref[...]))
    h = jnp.sin(fr * (_dot(h, w3_ref[...], HIGHEST) + b3_ref[...]))
    k = _dot(h, wo_ref[...], HIGHEST) * jnp.exp(-t * deltas_ref[...])
    k = jnp.where(m == seq, 0.0, k)
    k_ref[...] = k

    @pl.when(i == 0)
    def _():
        asum_ref[...] = jnp.zeros_like(asum_ref)

    asum_ref[...] += jnp.sum(jnp.abs(k), axis=0, keepdims=True)


def _filter_taps(seq, w1, b1, w2, b2, w3, b3, freq, w_out):
    bands, deltas, _ = _small_consts()
    rows = 1024
    nblk = 2 * seq // rows
    half = nblk // 2
    hid = FILTER_HIDDEN
    vec = lambda a: a.reshape(1, -1)
    small = lambda shape: pl.BlockSpec(shape, lambda i: (0, 0))
    return pl.pallas_call(
        functools.partial(_filter_kernel, seq=seq, rows=rows),
        name="filter_taps",
        grid=(nblk,),
        in_specs=[small((1, hid)), small((FILTER_BANDS, hid)), small((FILTER_BANDS, hid)), small((1, hid)),
                  small((hid, hid)), small((1, hid)), small((hid, hid)), small((1, hid)), small((1, hid)),
                  pl.BlockSpec((hid, D_HYENA), lambda i: (0, jnp.where(i >= half, 1, 0))),
                  small((1, FILTER_BANDS)), small((1, D_HYENA))],
        out_specs=[pl.BlockSpec((rows, D_HYENA), lambda i: (i, 0)),
                   pl.BlockSpec((1, D_HYENA), lambda i: (0, 0))],
        out_shape=[jax.ShapeDtypeStruct((2 * seq, D_HYENA), F32),
                   jax.ShapeDtypeStruct((1, D_HYENA), F32)],
        compiler_params=_cparams("arbitrary"),
    )(w1[0:1], w1[1:1 + FILTER_BANDS], w1[1 + FILTER_BANDS:], vec(b1), w2, vec(b2), w3, vec(b3), vec(freq),
      w_out, jnp.asarray(bands), jnp.asarray(deltas))


def _dft1_kernel(x_ref, f_ref, o_ref, *, nb, width):
    for j in range(nb):
        sl = slice(j * width, (j + 1) * width)
        o_ref[0, :, sl] = _dot(f_ref[j], x_ref[0, :, sl].astype(BF16))


def _dft_stage1(x, f1, nb):
    bsz, k, lanes = x.shape
    n2n, rows, _ = f1.shape
    width = lanes // n2n
    return pl.pallas_call(
        functools.partial(_dft1_kernel, nb=nb, width=width),
        name="dft1",
        grid=(bsz, n2n // nb),
        in_specs=[pl.BlockSpec((1, k, nb * width), lambda b, j: (b, 0, j)),
                  pl.BlockSpec((nb, rows, k), lambda b, j: (j, 0, 0))],
        out_specs=pl.BlockSpec((1, rows, nb * width), lambda b, j: (b, 0, j)),
        out_shape=jax.ShapeDtypeStruct((bsz, rows, lanes), F32),
        compiler_params=_cparams("parallel", "parallel"),
    )(x, f1)


def _filter_spec_kernel(a_ref, g2_ref, asum_ref, o_ref, *, kb, n2n):
    scale = 1.0 / asum_ref[...]
    for j in range(kb):
        a = jnp.concatenate([a_ref[0, 0, j], a_ref[0, 1, j]], axis=0).astype(BF16)
        b = _dot(g2_ref[...], a) * scale
        o_ref[0, j] = b[:n2n]
        o_ref[1, j] = b[n2n:]


def _filter_spectrum(af, g2, asum, kb):
    _, _, kp, n2n, c = af.shape
    return pl.pallas_call(
        functools.partial(_filter_spec_kernel, kb=kb, n2n=n2n),
        name="filter_spec",
        grid=(kp // kb,),
        in_specs=[pl.BlockSpec((1, 2, kb, n2n, c), lambda i: (0, 0, i, 0, 0)),
                  pl.BlockSpec((2 * n2n, 2 * n2n), lambda i: (0, 0)),
                  pl.BlockSpec((1, c), lambda i: (0, 0))],
        out_specs=pl.BlockSpec((2, kb, n2n, c), lambda i: (0, i, 0, 0)),
        out_shape=jax.ShapeDtypeStruct((2, kp, n2n, c), F32),
        compiler_params=_cparams("parallel"),
    )(af, g2, asum)


def _dft2_kernel(a_ref, ks_ref, g2_ref, g2i_ref, o_ref, *, kb, n2n):
    for j in range(kb):
        a = jnp.concatenate([a_ref[0, 0, j], a_ref[0, 1, j]], axis=0).astype(BF16)
        b = _dot(g2_ref[...], a)
        br, bi = b[:n2n], b[n2n:]
        kr, ki = ks_ref[0, j], ks_ref[1, j]
        p = jnp.concatenate([br * kr - bi * ki, br * ki + bi * kr], axis=0).astype(BF16)
        q = _dot(g2i_ref[...], p)
        o_ref[0, 0, j] = q[:n2n]
        o_ref[0, 1, j] = q[n2n:]


def _dft_stage2(a, kspec, g2, g2i, kb):
    bsz, _, kp, n2n, c = a.shape
    return pl.pallas_call(
        functools.partial(_dft2_kernel, kb=kb, n2n=n2n),
        name="dft2",
        grid=(kp // kb, bsz),
        in_specs=[pl.BlockSpec((1, 2, kb, n2n, c), lambda i, b: (b, 0, i, 0, 0)),
                  pl.BlockSpec((2, kb, n2n, c), lambda i, b: (0, i, 0, 0)),
                  pl.BlockSpec((2 * n2n, 2 * n2n), lambda i, b: (0, 0)),
                  pl.BlockSpec((2 * n2n, 2 * n2n), lambda i, b: (0, 0))],
        out_specs=pl.BlockSpec((1, 2, kb, n2n, c), lambda i, b: (b, 0, i, 0, 0)),
        out_shape=jax.ShapeDtypeStruct(a.shape, F32),
        compiler_params=_cparams("parallel", "parallel"),
    )(a, kspec, g2, g2i)


def _dft3_kernel(q_ref, m_ref, w_ref, x0_ref, d_ref, y_ref, *, nb, width):
    d = d_ref[...]
    for j in range(nb):
        sl = slice(j * width, (j + 1) * width)
        conv = _dot(m_ref[j], q_ref[0, :, sl].astype(BF16))
        y_ref[0, :, sl] = x0_ref[0, :, sl] * (conv + w_ref[0, :, sl] * d)


def _dft_stage3(q, minv, w, x0, d, nb):
    bsz, rows, lanes = q.shape
    n2n, h1, _ = minv.shape
    width = lanes // n2n
    data = pl.BlockSpec((1, h1, nb * width), lambda b, j: (b, 0, j))
    return pl.pallas_call(
        functools.partial(_dft3_kernel, nb=nb, width=width),
        name="dft3",
        grid=(bsz, n2n // nb),
        in_specs=[pl.BlockSpec((1, rows, nb * width), lambda b, j: (b, 0, j)),
                  pl.BlockSpec((nb, h1, rows), lambda b, j: (j, 0, 0)),
                  data, data,
                  pl.BlockSpec((1, width), lambda b, j: (0, 0))],
        out_specs=data,
        out_shape=jax.ShapeDtypeStruct((bsz, h1, lanes), F32),
        compiler_params=_cparams("parallel", "parallel"),
    )(q, minv, w, x0, d)


def _inproj_kernel(x_ref, xp_ref, xn_ref, mod_ref, g1_ref, win_ref, cw_ref, cb_ref, gq_ref, gk_ref, bd_ref,
                   w_out, x0_out, q_out, k_out, v_out, g_out, *, tm, nt):
    i = pl.program_id(1)
    mod = mod_ref[0]
    sh1 = mod[:, 0:D_MODEL]
    scale1 = g1_ref[...] * (1.0 + mod[:, D_MODEL:2 * D_MODEL])

    def norm_mod(x):
        ms = jnp.mean(x * x, axis=-1, keepdims=True)
        return ((x * lax.rsqrt(ms + EPS)) * scale1 + sh1).astype(BF16)

    u = norm_mod(x_ref[0])
    uh = norm_mod(jnp.concatenate([xp_ref[0], xn_ref[0]], axis=0))
    row = lax.broadcasted_iota(jnp.int32, (tm, 1), 0)

    def conv_chunk(c):
        cols = slice(c * D_HYENA, (c + 1) * D_HYENA)
        z = _dot(u, win_ref[:, cols])
        zh = _dot(uh, win_ref[:, cols])
        prev = jnp.where(i > 0, zh[7:8], 0.0)
        nxt = jnp.where(i < nt - 1, zh[8:9], 0.0)
        zm = jnp.where(row == 0, prev, pltpu.roll(z, 1, 0))
        zp = jnp.where(row == tm - 1, nxt, pltpu.roll(z, tm - 1, 0))
        return zm * cw_ref[0:1, cols] + z * cw_ref[1:2, cols] + zp * cw_ref[2:3, cols] + cb_ref[:, cols]

    x0_out[0] = conv_chunk(0)
    w_out[0] = conv_chunk(1) * conv_chunk(2)

    base = 3 * D_HYENA

    def head_norm(z, gain):
        ms = _dot((z * z).astype(BF16), bd_ref[...])
        return (z * lax.rsqrt(ms + EPS)) * gain

    zq = _dot(u, win_ref[:, base:base + ATTN_WIDTH])
    q_out[0] = head_norm(zq, gq_ref[...]) * (HEAD_DIM ** -0.5)
    zk = _dot(u, win_ref[:, base + ATTN_WIDTH:base + 2 * ATTN_WIDTH])
    k_out[0] = head_norm(zk, gk_ref[...])
    v_out[0] = _dot(u, win_ref[:, base + 2 * ATTN_WIDTH:base + 3 * ATTN_WIDTH])
    zg = _dot(u, win_ref[:, base + 3 * ATTN_WIDTH:])
    g_out[0] = 1.0 / (1.0 + jnp.exp(-zg))


def _in_projection(x, mod, norm1_g, w_in_bf16, conv_w, conv_b, q_norm_g, k_norm_g, tm):
    bsz, seq, _ = x.shape
    nt = seq // tm
    hb = tm // 8
    _, _, blockdiag = _small_consts()
    tile = lambda w: pl.BlockSpec((1, tm, w), lambda b, i: (b, i, 0))
    outs = [D_HYENA, D_HYENA, ATTN_WIDTH, ATTN_WIDTH, ATTN_WIDTH, 2 * D_MODEL]
    return pl.pallas_call(
        functools.partial(_inproj_kernel, tm=tm, nt=nt),
        name="inproj",
        grid=(bsz, nt),
        in_specs=[tile(D_MODEL),
                  pl.BlockSpec((1, 8, D_MODEL), lambda b, i: (b, jnp.maximum(i * hb - 1, 0), 0)),
                  pl.BlockSpec((1, 8, D_MODEL), lambda b, i: (b, jnp.minimum((i + 1) * hb, seq // 8 - 1), 0)),
                  pl.BlockSpec((1, 1, 6 * D_MODEL), lambda b, i: (b, 0, 0)),
                  _const_spec((1, D_MODEL)),
                  _const_spec((D_MODEL, IN_PROJ_WIDTH)),
                  _const_spec((3, 3 * D_HYENA)),
                  _const_spec((1, 3 * D_HYENA)),
                  _const_spec((1, ATTN_WIDTH)),
                  _const_spec((1, ATTN_WIDTH)),
                  _const_spec((ATTN_WIDTH, ATTN_WIDTH))],
        out_specs=[tile(w) for w in outs],
        out_shape=[jax.ShapeDtypeStruct((bsz, seq, w), F32) for w in outs],
        compiler_params=_cparams("parallel", "parallel"),
    )(x, x, x, mod, norm1_g.reshape(1, -1), w_in_bf16, conv_w, conv_b.reshape(1, -1),
      q_norm_g.reshape(1, -1), k_norm_g.reshape(1, -1), jnp.asarray(blockdiag))


def _attn_kernel(q_ref, kp_ref, kc_ref, kn_ref, vp_ref, vc_ref, vn_ref, tb_ref, o_ref, l_ref, *, tq, sub):
    t0 = pl.program_id(2) * tq
    kfull = jnp.concatenate([kp_ref[0], kc_ref[0], kn_ref[0]], axis=0).astype(BF16)
    vfull = jnp.concatenate([vp_ref[0], vc_ref[0], vn_ref[0]], axis=0).astype(BF16)
    lower = lax.broadcasted_iota(jnp.int32, (1, 2 * HEAD_DIM), 1) < HEAD_DIM
    for j in range(tq // QBLK):
        kpos = t0 - RADIUS + QBLK * j + lax.broadcasted_iota(jnp.int32, (1, 2 * QBLK), 1)
        valid = (kpos >= 0) & (kpos < sub)
        rows = slice(QBLK * j, QBLK * (j + 1))
        win = slice(QBLK * j, QBLK * (j + 2))
        for pair in range(HEADS_PER_GROUP // 2):
            lanes = slice(2 * HEAD_DIM * pair, 2 * HEAD_DIM * (pair + 1))
            q2 = q_ref[0, rows, lanes]
            k2 = kfull[win, lanes]
            v2 = vfull[win, lanes]
            res = []
            for hh in range(2):
                sel = lower if hh == 0 else jnp.logical_not(lower)
                qm = jnp.where(sel, q2, 0.0).astype(BF16)
                s = lax.dot_general(qm, k2, (((1,), (1,)), ((), ())), preferred_element_type=F32)
                s = jnp.where(valid, s + tb_ref[0, 2 * pair + hh], NEG_INF)
                m = jnp.max(s, axis=-1, keepdims=True)
                p = jnp.exp(s - m)
                den = jnp.sum(p, axis=-1, keepdims=True)
                res.append((_dot(p.astype(BF16), v2) / den, m + jnp.log(den)))
            o_ref[0, rows, lanes] = jnp.where(lower, res[0][0], res[1][0])
            l_ref[0, rows, lanes] = jnp.where(lower, res[0][1], res[1][1])


def _dilated_group(q, k, v, tb, gi, dil):
    bsz, seq, _ = q.shape
    sub = seq // dil
    tq = min(sub, 512)
    nq = sub // tq
    hb = tq // RADIUS
    nlane = ATTN_WIDTH // GROUP_WIDTH
    view = lambda a: a.reshape(bsz, sub, dil * ATTN_WIDTH)
    cur = pl.BlockSpec((1, tq, GROUP_WIDTH), lambda b, r, i: (b, i, r * nlane + gi))
    prev = pl.BlockSpec((1, RADIUS, GROUP_WIDTH),
                        lambda b, r, i: (b, jnp.maximum(i * hb - 1, 0), r * nlane + gi))
    nxt = pl.BlockSpec((1, RADIUS, GROUP_WIDTH),
                       lambda b, r, i: (b, jnp.minimum((i + 1) * hb, sub // RADIUS - 1), r * nlane + gi))
    out = pl.BlockSpec((1, tq, GROUP_WIDTH), lambda b, r, i: (b, i, r))
    qv, kv, vv = view(q), view(k), view(v)
    o, lse = pl.pallas_call(
        functools.partial(_attn_kernel, tq=tq, sub=sub),
        name="attn",
        grid=(bsz, dil, nq),
        in_specs=[cur, prev, cur, nxt, prev, cur, nxt,
                  pl.BlockSpec((1, HEADS_PER_GROUP, QBLK, 2 * QBLK), lambda b, r, i: (gi, 0, 0, 0))],
        out_specs=[out, out],
        out_shape=[jax.ShapeDtypeStruct((bsz, sub, dil * GROUP_WIDTH), F32)] * 2,
        compiler_params=_cparams("parallel", "parallel", "parallel"),
    )(qv, kv, kv, kv, vv, vv, vv, tb)
    return o.reshape(bsz, seq, GROUP_WIDTH), lse.reshape(bsz, seq, GROUP_WIDTH)


def _merge_kernel(x_ref, yhy_ref, o0_ref, o1_ref, o2_ref, l0_ref, l1_ref, l2_ref, g_ref, mod_ref, g2_ref,
                  whb_ref, wab_ref, wo_ref, wup_ref, wdn_ref, out_ref):
    mod = mod_ref[0]
    gt1 = mod[:, 2 * D_MODEL:3 * D_MODEL]
    sh2 = mod[:, 3 * D_MODEL:4 * D_MODEL]
    sc2 = mod[:, 4 * D_MODEL:5 * D_MODEL]
    gt2 = mod[:, 5 * D_MODEL:6 * D_MODEL]

    l0, l1, l2 = l0_ref[0], l1_ref[0], l2_ref[0]
    lm = jnp.maximum(jnp.maximum(l0, l1), l2)
    e0, e1, e2 = jnp.exp(l0 - lm), jnp.exp(l1 - lm), jnp.exp(l2 - lm)
    y_at = (e0 * o0_ref[0] + e1 * o1_ref[0] + e2 * o2_ref[0]) / (e0 + e1 + e2)

    g = g_ref[0]
    mix = g[:, :D_MODEL] * _dot(yhy_ref[0].astype(BF16), whb_ref[...]) \
        + g[:, D_MODEL:] * _dot(y_at.astype(BF16), wab_ref[...])
    h = x_ref[0] + gt1 * _dot(mix.astype(BF16), wo_ref[...])

    ms = jnp.mean(h * h, axis=-1, keepdims=True)
    u2 = (h * lax.rsqrt(ms + EPS)) * (g2_ref[...] * (1.0 + sc2)) + sh2
    up = jnp.maximum(_dot(u2.astype(BF16), wup_ref[...]), 0.0)
    ff = _dot((up * up).astype(BF16), wdn_ref[...])
    out_ref[0] = h + gt2 * ff


def _merge_mlp(x, y_hy, outs, lses, gates, mod, norm2_g, w_hy_br, w_at_br, w_out, w_up, w_down, tm):
    bsz, seq, _ = x.shape
    tile = lambda w: pl.BlockSpec((1, tm, w), lambda b, i: (b, i, 0))
    return pl.pallas_call(
        _merge_kernel,
        name="merge_mlp",
        grid=(bsz, seq // tm),
        in_specs=[tile(D_MODEL), tile(D_HYENA)] + [tile(GROUP_WIDTH)] * 6 + [tile(2 * D_MODEL),
                  pl.BlockSpec((1, 1, 6 * D_MODEL), lambda b, i: (b, 0, 0)),
                  _const_spec((1, D_MODEL)),
                  _const_spec((D_HYENA, D_MODEL)),
                  _const_spec((GROUP_WIDTH, D_MODEL)),
                  _const_spec((D_MODEL, D_MODEL)),
                  _const_spec((D_MODEL, D_FF)),
                  _const_spec((D_FF, D_MODEL))],
        out_specs=tile(D_MODEL),
        out_shape=jax.ShapeDtypeStruct(x.shape, F32),
        compiler_params=_cparams("parallel", "parallel"),
    )(x, y_hy, *outs, *lses, gates, mod, norm2_g.reshape(1, -1), w_hy_br, w_at_br, w_out, w_up, w_down)


def _long_conv_gate(w, x0, kspec, hyena_d, consts):
    bsz, seq, c = w.shape
    n2n = seq // DFT_H1
    nb1 = 8
    kb = 5
    a = _dft_stage1(w.reshape(bsz, DFT_H1, n2n * c), consts["f1_half"], nb1)
    a = a.reshape(bsz, 2, DFT_KP, n2n, c)
    qf = _dft_stage2(a, kspec, consts["g2"], consts["g2i"], kb)
    qf = qf.reshape(bsz, 2 * DFT_KP, n2n * c)
    y = _dft_stage3(qf, consts["minv"], w.reshape(bsz, DFT_H1, n2n * c), x0.reshape(bsz, DFT_H1, n2n * c),
                    hyena_d.reshape(1, c), nb1)
    return y.reshape(bsz, seq, c)


def _filter_spec(seq, consts, filt):
    taps, asum = _filter_taps(seq, *filt)
    n2n = seq // DFT_H1
    af = _dft_stage1(taps.reshape(1, DFT_N1, n2n * D_HYENA), consts["f1_full"], 8)
    return _filter_spectrum(af.reshape(1, 2, DFT_KP, n2n, D_HYENA), consts["g2"], asum, 5)


def _encoder_layer(x, mod, tb, p):
    bsz, seq, _ = x.shape
    consts = {k: jnp.asarray(v) for k, v in _dft_consts(seq).items()}
    mod3 = mod.reshape(bsz, 1, 6 * D_MODEL)
    w, x0, q, k, v, gates = _in_projection(x, mod3, p["norm1_g"], p["w_in"], p["conv_w"], p["conv_b"],
                                           p["q_norm_g"], p["k_norm_g"], tm=256)
    kspec = _filter_spec(seq, consts, p["filt"])
    y_hy = _long_conv_gate(w, x0, kspec, p["hyena_d"], consts)
    outs, lses = [], []
    for gi, (_, dil) in enumerate(DILATED_GROUPS):
        o, lse = _dilated_group(q, k, v, tb, gi, dil)
        outs.append(o)
        lses.append(lse)
    return _merge_mlp(x, y_hy, outs, lses, gates, mod3, p["norm2_g"], p["w_hy_br"], p["w_at_br"], p["w_out"],
                      p["w_up"], p["w_down"], tm=256)


def kernel(x_prompt, x_sample, c_prompt, c_sample, rel_bias, ada_w, ada_b, norm1_g, w_in, conv_w, conv_b,
           filt_w1, filt_b1, filt_w2, filt_b2, filt_w3, filt_b3, filt_freq, filt_w_out, hyena_d, q_norm_g,
           k_norm_g, w_hy_br, w_at_br, w_out, norm2_g, w_up, w_down):
    depth = ada_w.shape[0]
    tb = _bias_tables(rel_bias)
    y_prompt, y_sample = x_prompt, x_sample
    nbp = c_prompt.shape[0]
    for l in range(depth):
        p = dict(norm1_g=norm1_g[l], w_in=w_in[l].astype(BF16), conv_w=conv_w[l], conv_b=conv_b[l],
                 filt=(filt_w1[l], filt_b1[l], filt_w2[l], filt_b2[l], filt_w3[l], filt_b3[l], filt_freq[l],
                       filt_w_out[l]),
                 hyena_d=hyena_d[l], q_norm_g=q_norm_g[l], k_norm_g=k_norm_g[l],
                 w_hy_br=w_hy_br[l].astype(BF16), w_at_br=w_at_br[l].astype(BF16), w_out=w_out[l].astype(BF16),
                 norm2_g=norm2_g[l], w_up=w_up[l].astype(BF16), w_down=w_down[l].astype(BF16))
        mod = _modulation(jnp.concatenate([c_prompt, c_sample], axis=0), ada_w[l], ada_b[l])
        y_prompt = _encoder_layer(y_prompt, mod[:nbp], tb, p)
        y_sample = _encoder_layer(y_sample, mod[nbp:], tb, p)
    return (y_prompt, y_sample)
```

```python
import functools
import math

import numpy as np
import jax
import jax.numpy as jnp
from jax import lax
from jax.experimental import pallas as pl
from jax.experimental.pallas import tpu as pltpu

F32 = jnp.float32
BF16 = jnp.bfloat16
HIGHEST = lax.Precision.HIGHEST

D_MODEL = 1024
EPS = 1e-6
HEAD_DIM = 64
N_HEADS = 12
DILATED_GROUPS = ((128, 1), (512, 4), (2048, 16))
HEADS_PER_GROUP = 4
ATTN_WIDTH = 768
PAIR_WIDTH = 2 * HEAD_DIM
PAIRS_PER_GROUP = HEADS_PER_GROUP // 2
RADIUS = 64
NUM_BUCKETS = 32
MAX_DISTANCE = 1024
NEG_INF = -1e30
D_HYENA = 768
FILTER_BANDS = 16
FILTER_HIDDEN = 64
DECAY_TARGET = 1e-2
FAST_DECAY_PCT = 0.3
SLOW_DECAY_PCT = 1.5
D_FF = 4096
IN_PROJ_WIDTH = 6656

SUBLANES = 8
DFT_N1 = 128
DFT_H1 = 64
DFT_KP = 65
DFT_ROWS = DFT_KP * SUBLANES
QBLK = 128
VMEM_LIMIT = 56 * 1024 * 1024


def _cparams(*sem):
    return pltpu.CompilerParams(dimension_semantics=sem, vmem_limit_bytes=VMEM_LIMIT)


def _const_spec(shape):
    nd = len(shape)
    return pl.BlockSpec(shape, lambda *_: (0,) * nd, pipeline_mode=pl.Buffered(1))


def _dot(a, b, precision=None):
    return jnp.dot(a, b, preferred_element_type=F32, precision=precision)


@functools.lru_cache(maxsize=None)
def _dft_consts(L):
    n2n = L // DFT_H1
    n = 2 * L
    nch = n2n // SUBLANES
    r = np.arange(SUBLANES)
    eye = np.eye(SUBLANES)
    k1 = np.arange(DFT_KP)
    n1 = np.arange(DFT_H1)
    ang = -2.0 * np.pi * (r[:, None, None] * k1[None, :, None] / n + k1[None, :, None] * n1[None, None, :] / DFT_N1)
    f1 = np.stack([np.cos(ang), np.sin(ang)], axis=1)
    f0 = np.einsum("rpkn,rs->pkrns", f1, eye).reshape(2 * DFT_ROWS, DFT_H1 * SUBLANES)
    angc = 2.0 * np.pi * (n1[None, :, None] * k1[None, None, :] / DFT_N1 + r[:, None, None] * k1[None, None, :] / n)
    mult = np.full(DFT_KP, 2.0)
    mult[0] = mult[-1] = 1.0
    mi = np.stack([mult * np.cos(angc), -mult * np.sin(angc)], axis=2) / n
    m0 = np.einsum("rnpk,rs->nrpks", mi, eye).reshape(DFT_H1 * SUBLANES, 2 * DFT_ROWS)
    theta = -2.0 * np.pi * SUBLANES * np.arange(nch)[:, None] * k1[None, :] / n
    rep = lambda a: np.repeat(a, SUBLANES, axis=1)[:, :, None].astype(np.float32)
    sign = np.repeat(np.where(k1 % 2 == 0, 1.0, -1.0), SUBLANES)[:, None].astype(np.float32)
    a2 = -2.0 * np.pi * np.arange(n2n)[:, None] * np.arange(n2n)[None, :] / n2n
    fr, fi = np.cos(a2), np.sin(a2)
    g2 = np.block([[fr, -fi], [fi, fr]])
    g2i = np.block([[fr, fi], [-fi, fr]])
    as_bf16 = lambda a: np.asarray(a, np.float32).astype(BF16)
    return dict(f0=as_bf16(f0), m0=as_bf16(m0), rot_c=rep(np.cos(theta)), rot_s=rep(np.sin(theta)), sign=sign,
                g2=as_bf16(g2), g2i=as_bf16(g2i))


def _t5_bucket_np(rel):
    half = NUM_BUCKETS // 2
    max_exact = half // 2
    n = np.abs(rel)
    ret = np.where(rel > 0, half, 0)
    large = max_exact + (np.log(np.maximum(n, 1).astype(np.float32) / np.float32(max_exact))
                         / np.float32(math.log(MAX_DISTANCE / max_exact))
                         * np.float32(half - max_exact)).astype(np.int32)
    large = np.minimum(large, half - 1)
    return ret + np.where(n < max_exact, n, large)


@functools.lru_cache(maxsize=None)
def _bucket_tables():
    qi = np.arange(QBLK)[:, None]
    c = np.arange(2 * QBLK)[None, :]
    rel = c - RADIUS - qi
    tabs = []
    for _, dil in DILATED_GROUPS:
        b = _t5_bucket_np(rel * dil)
        tabs.append(np.where(np.abs(rel) <= RADIUS, b, -1))
    return np.stack(tabs).astype(np.int32)


@functools.lru_cache(maxsize=None)
def _small_consts():
    bands = np.linspace(1e-4, FILTER_BANDS - 1, FILTER_BANDS, dtype=np.float32)[None, :]
    deltas = np.abs(np.linspace(math.log(DECAY_TARGET) / SLOW_DECAY_PCT,
                                math.log(DECAY_TARGET) / FAST_DECAY_PCT, D_HYENA, dtype=np.float32))[None, :]
    head = np.arange(ATTN_WIDTH) // HEAD_DIM
    blockdiag = (head[:, None] == head[None, :]).astype(np.float32) / HEAD_DIM
    return bands, deltas, blockdiag.astype(BF16)


def _mod_kernel(c_ref, w_ref, b_ref, o_ref):
    c = c_ref[...]
    s = c / (1.0 + jnp.exp(-c))
    o_ref[...] = _dot(s, w_ref[...], HIGHEST) + b_ref[...]


def _modulation(c, ada_w, ada_b):
    nb, _ = c.shape
    nw = ada_w.shape[1]
    tn = 1024
    return pl.pallas_call(
        _mod_kernel,
        name="mod",
        grid=(nw // tn,),
        in_specs=[pl.BlockSpec((nb, D_MODEL), lambda j: (0, 0)),
                  pl.BlockSpec((D_MODEL, tn), lambda j: (0, j)),
                  pl.BlockSpec((1, tn), lambda j: (0, j))],
        out_specs=pl.BlockSpec((nb, tn), lambda j: (0, j)),
        out_shape=jax.ShapeDtypeStruct((nb, nw), F32),
        compiler_params=_cparams("parallel"),
    )(c, ada_w, ada_b.reshape(1, nw))


def _bias_kernel(rb_ref, bk_ref, o_ref):
    g = pl.program_id(0)
    bk = bk_ref[0]
    for h in range(HEADS_PER_GROUP):
        acc = jnp.full(bk.shape, NEG_INF, F32)
        for b in range(NUM_BUCKETS):
            acc = jnp.where(bk == b, rb_ref[b, g * HEADS_PER_GROUP + h], acc)
        o_ref[0, h] = acc


def _bias_tables(rel_bias):
    bk = jnp.asarray(_bucket_tables())
    ng = len(DILATED_GROUPS)
    return pl.pallas_call(
        _bias_kernel,
        name="bias_tab",
        grid=(ng,),
        in_specs=[pl.BlockSpec(memory_space=pltpu.SMEM),
                  pl.BlockSpec((1, QBLK, 2 * QBLK), lambda g: (g, 0, 0))],
        out_specs=pl.BlockSpec((1, HEADS_PER_GROUP, QBLK, 2 * QBLK), lambda g: (g, 0, 0, 0)),
        out_shape=jax.ShapeDtypeStruct((ng, HEADS_PER_GROUP, QBLK, 2 * QBLK), F32),
        compiler_params=_cparams("arbitrary"),
    )(rel_bias, bk)


def _filter_kernel(w1t_ref, w1c_ref, w1s_ref, b1_ref, w2_ref, b2_ref, w3_ref, b3_ref, fr_ref, wo_ref,
                   bands_ref, deltas_ref, k_ref, asum_ref, *, seq, rows):
    i = pl.program_id(0)
    m = i * rows + lax.broadcasted_iota(jnp.int32, (rows, 1), 0)
    pos = jnp.where(m < seq, m, 2 * seq - m).astype(F32)
    t = pos / float(seq - 1)
    w = (2.0 * math.pi) * pos / float(seq)
    arg = w * bands_ref[...]
    fr = fr_ref[...]
    z = t * w1t_ref[...] + _dot(jnp.cos(arg), w1c_ref[...], HIGHEST) \
        + _dot(-jnp.sin(arg), w1s_ref[...], HIGHEST) + b1_ref[...]
    h = jnp.sin(fr * z)
    h = jnp.sin(fr * (_dot(h, w2_ref[...], HIGHEST) + b2_ref[...]))
    h = jnp.sin(fr * (_dot(h, w3_ref[...], HIGHEST) + b3_ref[...]))
    k = _dot(h, wo_ref[...], HIGHEST) * jnp.exp(-t * deltas_ref[...])
    k = jnp.where(m == seq, 0.0, k)
    k_ref[...] = k

    @pl.when(i == 0)
    def _():
        asum_ref[...] = jnp.zeros_like(asum_ref)

    asum_ref[...] += jnp.sum(jnp.abs(k), axis=0, keepdims=True)


def _filter_taps(seq, w1, b1, w2, b2, w3, b3, freq, w_out):
    bands, deltas, _ = _small_consts()
    rows = 1024
    nblk = 2 * seq // rows
    half = nblk // 2
    hid = FILTER_HIDDEN
    vec = lambda a: a.reshape(1, -1)
    small = lambda shape: pl.BlockSpec(shape, lambda i: (0, 0))
    return pl.pallas_call(
        functools.partial(_filter_kernel, seq=seq, rows=rows),
        name="filter_taps",
        grid=(nblk,),
        in_specs=[small((1, hid)), small((FILTER_BANDS, hid)), small((FILTER_BANDS, hid)), small((1, hid)),
                  small((hid, hid)), small((1, hid)), small((hid, hid)), small((1, hid)), small((1, hid)),
                  pl.BlockSpec((hid, D_HYENA), lambda i: (0, jnp.where(i >= half, 1, 0))),
                  small((1, FILTER_BANDS)), small((1, D_HYENA))],
        out_specs=[pl.BlockSpec((rows, D_HYENA), lambda i: (i, 0)),
                   pl.BlockSpec((1, D_HYENA), lambda i: (0, 0))],
        out_shape=[jax.ShapeDtypeStruct((2 * seq, D_HYENA), F32),
                   jax.ShapeDtypeStruct((1, D_HYENA), F32)],
        compiler_params=_cparams("arbitrary"),
    )(w1[0:1], w1[1:1 + FILTER_BANDS], w1[1 + FILTER_BANDS:], vec(b1), w2, vec(b2), w3, vec(b3), vec(freq),
      w_out, jnp.asarray(bands), jnp.asarray(deltas))


def _rotate(re, im, c, s):
    return c * re - s * im, s * re + c * im


def _store_spectrum_block(o_ref, re, im):
    c = re.shape[-1]
    o_ref[0, 0] = re.reshape(DFT_KP, SUBLANES, c)
    o_ref[0, 1] = im.reshape(DFT_KP, SUBLANES, c)


def _dft1_kernel(x_ref, f0_ref, rc_ref, rs_ref, o_ref):
    c = x_ref.shape[-1]
    x = x_ref[0].reshape(DFT_H1 * SUBLANES, c).astype(BF16)
    a = _dot(f0_ref[...], x)
    re, im = _rotate(a[:DFT_ROWS], a[DFT_ROWS:], rc_ref[0], rs_ref[0])
    _store_spectrum_block(o_ref, re, im)


def _dft1_filter_kernel(lo_ref, hi_ref, f0_ref, rc_ref, rs_ref, sign_ref, o_ref):
    c = lo_ref.shape[-1]
    flat = lambda ref: ref[0].reshape(DFT_H1 * SUBLANES, c).astype(BF16)
    lo = _dot(f0_ref[...], flat(lo_ref))
    hi = _dot(f0_ref[...], flat(hi_ref))
    sign = sign_ref[...]
    re, im = _rotate(lo[:DFT_ROWS] + sign * hi[:DFT_ROWS], lo[DFT_ROWS:] + sign * hi[DFT_ROWS:],
                     rc_ref[0], rs_ref[0])
    _store_spectrum_block(o_ref, re, im)


def _dft_stage1(x, consts, is_filter):
    bsz, rows, c = x.shape
    n2n = rows // (DFT_N1 if is_filter else DFT_H1)
    nch = n2n // SUBLANES
    rot = pl.BlockSpec((1, DFT_ROWS, 1), lambda j, b: (j, 0, 0))
    f0 = _const_spec((2 * DFT_ROWS, DFT_H1 * SUBLANES))
    if is_filter:
        xv = x.reshape(2, DFT_H1, n2n, c)
        data = [pl.BlockSpec((1, DFT_H1, SUBLANES, c), lambda j, b: (0, 0, j, 0)),
                pl.BlockSpec((1, DFT_H1, SUBLANES, c), lambda j, b: (1, 0, j, 0))]
        args = (xv, xv, consts["f0"], consts["rot_c"], consts["rot_s"], consts["sign"])
        specs = data + [f0, rot, rot, _const_spec((DFT_ROWS, 1))]
        body = _dft1_filter_kernel
    else:
        xv = x.reshape(bsz, DFT_H1, n2n, c)
        args = (xv, consts["f0"], consts["rot_c"], consts["rot_s"])
        specs = [pl.BlockSpec((1, DFT_H1, SUBLANES, c), lambda j, b: (b, 0, j, 0)), f0, rot, rot]
        body = _dft1_kernel
    return pl.pallas_call(
        body,
        name="dft1",
        grid=(nch, bsz),
        in_specs=specs,
        out_specs=pl.BlockSpec((1, 2, DFT_KP, SUBLANES, c), lambda j, b: (b, 0, 0, j, 0)),
        out_shape=jax.ShapeDtypeStruct((bsz, 2, DFT_KP, n2n, c), F32),
        compiler_params=_cparams("parallel", "parallel"),
    )(*args)


def _filter_spec_kernel(a_ref, g2_ref, asum_ref, o_ref, *, kb, n2n):
    scale = 1.0 / asum_ref[...]
    for j in range(kb):
        a = jnp.concatenate([a_ref[0, 0, j], a_ref[0, 1, j]], axis=0).astype(BF16)
        b = _dot(g2_ref[...], a) * scale
        o_ref[0, j] = b[:n2n]
        o_ref[1, j] = b[n2n:]


def _filter_spectrum(af, g2, asum, kb):
    _, _, kp, n2n, c = af.shape
    return pl.pallas_call(
        functools.partial(_filter_spec_kernel, kb=kb, n2n=n2n),
        name="filter_spec",
        grid=(kp // kb,),
        in_specs=[pl.BlockSpec((1, 2, kb, n2n, c), lambda i: (0, 0, i, 0, 0)),
                  pl.BlockSpec((2 * n2n, 2 * n2n), lambda i: (0, 0)),
                  pl.BlockSpec((1, c), lambda i: (0, 0))],
        out_specs=pl.BlockSpec((2, kb, n2n, c), lambda i: (0, i, 0, 0)),
        out_shape=jax.ShapeDtypeStruct((2, kp, n2n, c), F32),
        compiler_params=_cparams("parallel"),
    )(af, g2, asum)


def _dft2_kernel(a_ref, ks_ref, g2_ref, g2i_ref, o_ref, *, kb, n2n):
    for j in range(kb):
        a = jnp.concatenate([a_ref[0, 0, j], a_ref[0, 1, j]], axis=0).astype(BF16)
        b = _dot(g2_ref[...], a)
        br, bi = b[:n2n], b[n2n:]
        kr, ki = ks_ref[0, j], ks_ref[1, j]
        p = jnp.concatenate([br * kr - bi * ki, br * ki + bi * kr], axis=0).astype(BF16)
        q = _dot(g2i_ref[...], p)
        o_ref[0, 0, j] = q[:n2n]
        o_ref[0, 1, j] = q[n2n:]


def _dft_stage2(a, kspec, g2, g2i, kb):
    bsz, _, kp, n2n, c = a.shape
    return pl.pallas_call(
        functools.partial(_dft2_kernel, kb=kb, n2n=n2n),
        name="dft2",
        grid=(kp // kb, bsz),
        in_specs=[pl.BlockSpec((1, 2, kb, n2n, c), lambda i, b: (b, 0, i, 0, 0)),
                  pl.BlockSpec((2, kb, n2n, c), lambda i, b: (0, i, 0, 0)),
                  pl.BlockSpec((2 * n2n, 2 * n2n), lambda i, b: (0, 0)),
                  pl.BlockSpec((2 * n2n, 2 * n2n), lambda i, b: (0, 0))],
        out_specs=pl.BlockSpec((1, 2, kb, n2n, c), lambda i, b: (b, 0, i, 0, 0)),
        out_shape=jax.ShapeDtypeStruct(a.shape, F32),
        compiler_params=_cparams("parallel", "parallel"),
    )(a, kspec, g2, g2i)


def _dft3_kernel(q_ref, m0_ref, rc_ref, rs_ref, w_ref, x0_ref, d_ref, y_ref):
    c = w_ref.shape[-1]
    qr = q_ref[0, 0].reshape(DFT_ROWS, c)
    qi = q_ref[0, 1].reshape(DFT_ROWS, c)
    re, im = _rotate(qr, qi, rc_ref[0], -rs_ref[0])
    q = jnp.concatenate([re, im], axis=0).astype(BF16)
    conv = _dot(m0_ref[...], q).reshape(DFT_H1, SUBLANES, c)
    y_ref[0] = x0_ref[0] * (conv + w_ref[0] * d_ref[...])


def _dft_stage3(q, consts, w, x0, d):
    bsz, seq, c = w.shape
    n2n = seq // DFT_H1
    nch = n2n // SUBLANES
    view = lambda a: a.reshape(bsz, DFT_H1, n2n, c)
    data = pl.BlockSpec((1, DFT_H1, SUBLANES, c), lambda j, b: (b, 0, j, 0))
    rot = pl.BlockSpec((1, DFT_ROWS, 1), lambda j, b: (j, 0, 0))
    y = pl.pallas_call(
        _dft3_kernel,
        name="dft3",
        grid=(nch, bsz),
        in_specs=[pl.BlockSpec((1, 2, DFT_KP, SUBLANES, c), lambda j, b: (b, 0, 0, j, 0)),
                  _const_spec((DFT_H1 * SUBLANES, 2 * DFT_ROWS)), rot, rot, data, data,
                  _const_spec((1, c))],
        out_specs=data,
        out_shape=jax.ShapeDtypeStruct((bsz, DFT_H1, n2n, c), F32),
        compiler_params=_cparams("parallel", "parallel"),
    )(q, consts["m0"], consts["rot_c"], consts["rot_s"], view(w), view(x0), d)
    return y.reshape(bsz, seq, c)


def _inproj_kernel(x_ref, xp_ref, xn_ref, mod_ref, g1_ref, win_ref, cw_ref, cb_ref, gq_ref, gk_ref, bd_ref,
                   w_out, x0_out, q_out, k_out, v_out, g_out, *, tm, nt):
    i = pl.program_id(1)
    mod = mod_ref[0]
    sh1 = mod[:, 0:D_MODEL]
    scale1 = g1_ref[...] * (1.0 + mod[:, D_MODEL:2 * D_MODEL])

    def norm_mod(x):
        ms = jnp.mean(x * x, axis=-1, keepdims=True)
        return ((x * lax.rsqrt(ms + EPS)) * scale1 + sh1).astype(BF16)

    u = norm_mod(x_ref[0])
    uh = norm_mod(jnp.concatenate([xp_ref[0], xn_ref[0]], axis=0))
    row = lax.broadcasted_iota(jnp.int32, (tm, 1), 0)

    def conv_chunk(c):
        cols = slice(c * D_HYENA, (c + 1) * D_HYENA)
        z = _dot(u, win_ref[:, cols])
        zh = _dot(uh, win_ref[:, cols])
        prev = jnp.where(i > 0, zh[7:8], 0.0)
        nxt = jnp.where(i < nt - 1, zh[8:9], 0.0)
        zm = jnp.where(row == 0, prev, pltpu.roll(z, 1, 0))
        zp = jnp.where(row == tm - 1, nxt, pltpu.roll(z, tm - 1, 0))
        return zm * cw_ref[0:1, cols] + z * cw_ref[1:2, cols] + zp * cw_ref[2:3, cols] + cb_ref[:, cols]

    x0_out[0] = conv_chunk(0)
    w_out[0] = conv_chunk(1) * conv_chunk(2)

    base = 3 * D_HYENA

    def head_norm(z, gain):
        ms = _dot((z * z).astype(BF16), bd_ref[...])
        return (z * lax.rsqrt(ms + EPS)) * gain

    zq = _dot(u, win_ref[:, base:base + ATTN_WIDTH])
    q_out[0] = head_norm(zq, gq_ref[...]) * (HEAD_DIM ** -0.5)
    zk = _dot(u, win_ref[:, base + ATTN_WIDTH:base + 2 * ATTN_WIDTH])
    k_out[0] = head_norm(zk, gk_ref[...])
    v_out[0] = _dot(u, win_ref[:, base + 2 * ATTN_WIDTH:base + 3 * ATTN_WIDTH])
    zg = _dot(u, win_ref[:, base + 3 * ATTN_WIDTH:])
    g_out[0] = 1.0 / (1.0 + jnp.exp(-zg))


def _in_projection(x, mod, norm1_g, w_in_bf16, conv_w, conv_b, q_norm_g, k_norm_g, tm):
    bsz, seq, _ = x.shape
    nt = seq // tm
    hb = tm // SUBLANES
    _, _, blockdiag = _small_consts()
    tile = lambda w: pl.BlockSpec((1, tm, w), lambda b, i: (b, i, 0))
    outs = [D_HYENA, D_HYENA, ATTN_WIDTH, ATTN_WIDTH, ATTN_WIDTH, 2 * D_MODEL]
    return pl.pallas_call(
        functools.partial(_inproj_kernel, tm=tm, nt=nt),
        name="inproj",
        grid=(bsz, nt),
        in_specs=[tile(D_MODEL),
                  pl.BlockSpec((1, SUBLANES, D_MODEL), lambda b, i: (b, jnp.maximum(i * hb - 1, 0), 0)),
                  pl.BlockSpec((1, SUBLANES, D_MODEL),
                               lambda b, i: (b, jnp.minimum((i + 1) * hb, seq // SUBLANES - 1), 0)),
                  pl.BlockSpec((1, 1, 6 * D_MODEL), lambda b, i: (b, 0, 0)),
                  _const_spec((1, D_MODEL)),
                  _const_spec((D_MODEL, IN_PROJ_WIDTH)),
                  _const_spec((3, 3 * D_HYENA)),
                  _const_spec((1, 3 * D_HYENA)),
                  _const_spec((1, ATTN_WIDTH)),
                  _const_spec((1, ATTN_WIDTH)),
                  _const_spec((ATTN_WIDTH, ATTN_WIDTH))],
        out_specs=[tile(w) for w in outs],
        out_shape=[jax.ShapeDtypeStruct((bsz, seq, w), F32) for w in outs],
        compiler_params=_cparams("parallel", "parallel"),
    )(x, x, x, mod, norm1_g.reshape(1, -1), w_in_bf16, conv_w, conv_b.reshape(1, -1),
      q_norm_g.reshape(1, -1), k_norm_g.reshape(1, -1), jnp.asarray(blockdiag))


def _attn_kernel(*refs, tq, sub, dil):
    npair = PAIRS_PER_GROUP
    q_refs = refs[0:npair]
    k_refs = refs[npair:4 * npair]
    v_refs = refs[4 * npair:7 * npair]
    tb_ref = refs[7 * npair]
    o_refs = refs[7 * npair + 1:8 * npair + 1]
    l_refs = refs[8 * npair + 1:9 * npair + 1]
    t0 = pl.program_id(1) * tq
    lower = lax.broadcasted_iota(jnp.int32, (1, PAIR_WIDTH), 1) < HEAD_DIM

    def rows(r, count, offset=0):
        if dil == 1:
            return pl.ds(offset, count)
        return pl.ds(r + offset * dil, count, stride=dil)

    def residue(r):
        for pair in range(npair):
            kp, kc, kn = k_refs[3 * pair:3 * pair + 3]
            vp, vc, vn = v_refs[3 * pair:3 * pair + 3]
            kfull = jnp.concatenate([kp[0, rows(r, RADIUS), :], kc[0, rows(r, tq), :], kn[0, rows(r, RADIUS), :]],
                                    axis=0).astype(BF16)
            vfull = jnp.concatenate([vp[0, rows(r, RADIUS), :], vc[0, rows(r, tq), :], vn[0, rows(r, RADIUS), :]],
                                    axis=0).astype(BF16)
            for j in range(tq // QBLK):
                kpos = t0 - RADIUS + QBLK * j + lax.broadcasted_iota(jnp.int32, (1, 2 * QBLK), 1)
                valid = (kpos >= 0) & (kpos < sub)
                win = slice(QBLK * j, QBLK * (j + 2))
                q2 = q_refs[pair][0, rows(r, QBLK, QBLK * j), :]
                k2 = kfull[win]
                v2 = vfull[win]
                res = []
                for hh in range(2):
                    sel = lower if hh == 0 else jnp.logical_not(lower)
                    qm = jnp.where(sel, q2, 0.0).astype(BF16)
                    s = lax.dot_general(qm, k2, (((1,), (1,)), ((), ())), preferred_element_type=F32)
                    s = jnp.where(valid, s + tb_ref[0, 2 * pair + hh], NEG_INF)
                    m = jnp.max(s, axis=-1, keepdims=True)
                    p = jnp.exp(s - m)
                    den = jnp.sum(p, axis=-1, keepdims=True)
                    res.append((_dot(p.astype(BF16), v2) / den, m + jnp.log(den)))
                o_refs[pair][0, rows(r, QBLK, QBLK * j), :] = jnp.where(lower, res[0][0], res[1][0])
                l_refs[pair][0, rows(r, QBLK, QBLK * j), :] = jnp.where(lower, res[0][1], res[1][1])

    if dil == 1:
        residue(0)
    else:
        def body(r, carry):
            residue(r)
            return carry
        lax.fori_loop(0, dil, body, 0)


def _dilated_group(q, k, v, tb, gi, dil, tq):
    bsz, seq, _ = q.shape
    sub = seq // dil
    tq = min(tq, sub)
    tp = tq * dil
    halo = RADIUS * dil
    hb = tp // halo
    lane0 = gi * PAIRS_PER_GROUP
    cur = lambda p: pl.BlockSpec((1, tp, PAIR_WIDTH), lambda b, i: (b, i, lane0 + p))
    prev = lambda p: pl.BlockSpec((1, halo, PAIR_WIDTH), lambda b, i: (b, jnp.maximum(i * hb - 1, 0), lane0 + p))
    nxt = lambda p: pl.BlockSpec((1, halo, PAIR_WIDTH),
                                 lambda b, i: (b, jnp.minimum((i + 1) * hb, seq // halo - 1), lane0 + p))
    out = pl.BlockSpec((1, tp, PAIR_WIDTH), lambda b, i: (b, i, 0))
    pairs = range(PAIRS_PER_GROUP)
    halo_specs = [s(p) for p in pairs for s in (prev, cur, nxt)]
    res = pl.pallas_call(
        functools.partial(_attn_kernel, tq=tq, sub=sub, dil=dil),
        name="attn",
        grid=(bsz, seq // tp),
        in_specs=[cur(p) for p in pairs] + halo_specs + halo_specs
                 + [pl.BlockSpec((1, HEADS_PER_GROUP, QBLK, 2 * QBLK), lambda b, i: (gi, 0, 0, 0))],
        out_specs=[out] * (2 * PAIRS_PER_GROUP),
        out_shape=[jax.ShapeDtypeStruct((bsz, seq, PAIR_WIDTH), F32)] * (2 * PAIRS_PER_GROUP),
        compiler_params=_cparams("parallel", "parallel"),
    )(*([q] * PAIRS_PER_GROUP + [k] * (3 * PAIRS_PER_GROUP) + [v] * (3 * PAIRS_PER_GROUP) + [tb]))
    return res[:PAIRS_PER_GROUP], res[PAIRS_PER_GROUP:]


def _merge_kernel(*refs):
    ng = len(DILATED_GROUPS)
    npair = PAIRS_PER_GROUP
    x_ref, yhy_ref = refs[0:2]
    o_refs = refs[2:2 + ng * npair]
    l_refs = refs[2 + ng * npair:2 + 2 * ng * npair]
    g_ref, mod_ref, g2_ref, whb_ref, wab_ref, wo_ref, wup_ref, wdn_ref, out_ref = refs[2 + 2 * ng * npair:]
    mod = mod_ref[0]
    gt1 = mod[:, 2 * D_MODEL:3 * D_MODEL]
    sh2 = mod[:, 3 * D_MODEL:4 * D_MODEL]
    sc2 = mod[:, 4 * D_MODEL:5 * D_MODEL]
    gt2 = mod[:, 5 * D_MODEL:6 * D_MODEL]

    merged = []
    for p in range(npair):
        lses = [l_refs[g * npair + p][0] for g in range(ng)]
        lm = functools.reduce(jnp.maximum, lses)
        es = [jnp.exp(l - lm) for l in lses]
        num = sum(e * o_refs[g * npair + p][0] for g, e in enumerate(es))
        merged.append(num / sum(es))
    y_at = jnp.concatenate(merged, axis=-1)

    g = g_ref[0]
    mix = g[:, :D_MODEL] * _dot(yhy_ref[0].astype(BF16), whb_ref[...]) \
        + g[:, D_MODEL:] * _dot(y_at.astype(BF16), wab_ref[...])
    h = x_ref[0] + gt1 * _dot(mix.astype(BF16), wo_ref[...])

    ms = jnp.mean(h * h, axis=-1, keepdims=True)
    u2 = (h * lax.rsqrt(ms + EPS)) * (g2_ref[...] * (1.0 + sc2)) + sh2
    up = jnp.maximum(_dot(u2.astype(BF16), wup_ref[...]), 0.0)
    ff = _dot((up * up).astype(BF16), wdn_ref[...])
    out_ref[0] = h + gt2 * ff


def _merge_mlp(x, y_hy, outs, lses, gates, mod, norm2_g, w_hy_br, w_at_br, w_out, w_up, w_down, tm):
    bsz, seq, _ = x.shape
    tile = lambda w: pl.BlockSpec((1, tm, w), lambda b, i: (b, i, 0))
    n_attn = len(outs) + len(lses)
    return pl.pallas_call(
        _merge_kernel,
        name="merge_mlp",
        grid=(bsz, seq // tm),
        in_specs=[tile(D_MODEL), tile(D_HYENA)] + [tile(PAIR_WIDTH)] * n_attn + [tile(2 * D_MODEL),
                  pl.BlockSpec((1, 1, 6 * D_MODEL), lambda b, i: (b, 0, 0)),
                  _const_spec((1, D_MODEL)),
                  _const_spec((D_HYENA, D_MODEL)),
                  _const_spec((PAIRS_PER_GROUP * PAIR_WIDTH, D_MODEL)),
                  _const_spec((D_MODEL, D_MODEL)),
                  _const_spec((D_MODEL, D_FF)),
                  _const_spec((D_FF, D_MODEL))],
        out_specs=tile(D_MODEL),
        out_shape=jax.ShapeDtypeStruct(x.shape, F32),
        compiler_params=_cparams("parallel", "parallel"),
    )(x, y_hy, *outs, *lses, gates, mod, norm2_g.reshape(1, -1), w_hy_br, w_at_br, w_out, w_up, w_down)


ATTN_TQ = {1: 512, 4: 256, 16: 128}


def _long_conv_gate(w, x0, kspec, hyena_d, consts):
    a = _dft_stage1(w, consts, is_filter=False)
    qf = _dft_stage2(a, kspec, consts["g2"], consts["g2i"], 5)
    return _dft_stage3(qf, consts, w, x0, hyena_d.reshape(1, -1))


def _filter_spec(seq, consts, filt):
    taps, asum = _filter_taps(seq, *filt)
    af = _dft_stage1(taps[None], consts, is_filter=True)
    return _filter_spectrum(af, consts["g2"], asum, 5)


def _encoder_layer(x, mod, tb, p):
    bsz, seq, _ = x.shape
    consts = {k: jnp.asarray(v) for k, v in _dft_consts(seq).items()}
    mod3 = mod.reshape(bsz, 1, 6 * D_MODEL)
    w, x0, q, k, v, gates = _in_projection(x, mod3, p["norm1_g"], p["w_in"], p["conv_w"], p["conv_b"],
                                           p["q_norm_g"], p["k_norm_g"], tm=256)
    kspec = _filter_spec(seq, consts, p["filt"])
    y_hy = _long_conv_gate(w, x0, kspec, p["hyena_d"], consts)
    outs, lses = [], []
    for gi, (_, dil) in enumerate(DILATED_GROUPS):
        o, lse = _dilated_group(q, k, v, tb, gi, dil, ATTN_TQ[dil])
        outs.extend(o)
        lses.extend(lse)
    return _merge_mlp(x, y_hy, outs, lses, gates, mod3, p["norm2_g"], p["w_hy_br"], p["w_at_br"], p["w_out"],
                      p["w_up"], p["w_down"], tm=256)


def kernel(x_prompt, x_sample, c_prompt, c_sample, rel_bias, ada_w, ada_b, norm1_g, w_in, conv_w, conv_b,
           filt_w1, filt_b1, filt_w2, filt_b2, filt_w3, filt_b3, filt_freq, filt_w_out, hyena_d, q_norm_g,
           k_norm_g, w_hy_br, w_at_br, w_out, norm2_g, w_up, w_down):
    depth = ada_w.shape[0]
    tb = _bias_tables(rel_bias)
    y_prompt, y_sample = x_prompt, x_sample
    nbp = c_prompt.shape[0]
    for l in range(depth):
        p = dict(norm1_g=norm1_g[l], w_in=w_in[l].astype(BF16), conv_w=conv_w[l], conv_b=conv_b[l],
                 filt=(filt_w1[l], filt_b1[l], filt_w2[l], filt_b2[l], filt_w3[l], filt_b3[l], filt_freq[l],
                       filt_w_out[l]),
                 hyena_d=hyena_d[l], q_norm_g=q_norm_g[l], k_norm_g=k_norm_g[l],
                 w_hy_br=w_hy_br[l].astype(BF16), w_at_br=w_at_br[l].astype(BF16), w_out=w_out[l].astype(BF16),
                 norm2_g=norm2_g[l], w_up=w_up[l].astype(BF16), w_down=w_down[l].astype(BF16))
        mod = _modulation(jnp.concatenate([c_prompt, c_sample], axis=0), ada_w[l], ada_b[l])
        y_prompt = _encoder_layer(y_prompt, mod[:nbp], tb, p)
        y_sample = _encoder_layer(y_sample, mod[nbp:], tb, p)
    return (y_prompt, y_sample)
```

```python
import functools
import math

import numpy as np
import jax
import jax.numpy as jnp
from jax import lax
from jax.experimental import pallas as pl
from jax.experimental.pallas import tpu as pltpu

F32 = jnp.float32
BF16 = jnp.bfloat16
HIGHEST = lax.Precision.HIGHEST

D_MODEL = 1024
EPS = 1e-6
HEAD_DIM = 64
N_HEADS = 12
DILATED_GROUPS = ((128, 1), (512, 4), (2048, 16))
HEADS_PER_GROUP = 4
ATTN_WIDTH = 768
PAIR_WIDTH = 2 * HEAD_DIM
PAIRS_PER_GROUP = HEADS_PER_GROUP // 2
RADIUS = 64
NUM_BUCKETS = 32
MAX_DISTANCE = 1024
NEG_INF = -1e30
D_HYENA = 768
FILTER_BANDS = 16
FILTER_HIDDEN = 64
DECAY_TARGET = 1e-2
FAST_DECAY_PCT = 0.3
SLOW_DECAY_PCT = 1.5
D_FF = 4096
IN_PROJ_WIDTH = 6656

SUBLANES = 8
DFT_N1 = 128
DFT_H1 = 64
DFT_KP = 65
DFT_ROWS = DFT_KP * SUBLANES
QBLK = 128
VMEM_LIMIT = 56 * 1024 * 1024


def _cparams(*sem):
    return pltpu.CompilerParams(dimension_semantics=sem, vmem_limit_bytes=VMEM_LIMIT)


def _const_spec(shape):
    nd = len(shape)
    return pl.BlockSpec(shape, lambda *_: (0,) * nd, pipeline_mode=pl.Buffered(1))


def _dot(a, b, precision=None):
    return jnp.dot(a, b, preferred_element_type=F32, precision=precision)


@functools.lru_cache(maxsize=None)
def _dft_consts(L):
    n2n = L // DFT_H1
    n = 2 * L
    nch = n2n // SUBLANES
    r = np.arange(SUBLANES)
    eye = np.eye(SUBLANES)
    k1 = np.arange(DFT_KP)
    n1 = np.arange(DFT_H1)
    ang = -2.0 * np.pi * (r[:, None, None] * k1[None, :, None] / n + k1[None, :, None] * n1[None, None, :] / DFT_N1)
    f1 = np.stack([np.cos(ang), np.sin(ang)], axis=1)
    f0 = np.einsum("rpkn,rs->pkrns", f1, eye).reshape(2 * DFT_ROWS, DFT_H1 * SUBLANES)
    angc = 2.0 * np.pi * (n1[None, :, None] * k1[None, None, :] / DFT_N1 + r[:, None, None] * k1[None, None, :] / n)
    mult = np.full(DFT_KP, 2.0)
    mult[0] = mult[-1] = 1.0
    mi = np.stack([mult * np.cos(angc), -mult * np.sin(angc)], axis=2) / n
    m0 = np.einsum("rnpk,rs->nrpks", mi, eye).reshape(DFT_H1 * SUBLANES, 2 * DFT_ROWS)
    theta = -2.0 * np.pi * SUBLANES * np.arange(nch)[:, None] * k1[None, :] / n
    rep = lambda a: np.repeat(a, SUBLANES, axis=1)[:, :, None].astype(np.float32)
    sign = np.repeat(np.where(k1 % 2 == 0, 1.0, -1.0), SUBLANES)[:, None].astype(np.float32)
    a2 = -2.0 * np.pi * np.arange(n2n)[:, None] * np.arange(n2n)[None, :] / n2n
    fr, fi = np.cos(a2), np.sin(a2)
    g2 = np.block([[fr, -fi], [fi, fr]])
    g2i = np.block([[fr, fi], [-fi, fr]])
    as_bf16 = lambda a: np.asarray(a, np.float32).astype(BF16)
    return dict(f0=as_bf16(f0), m0=as_bf16(m0), rot_c=rep(np.cos(theta)), rot_s=rep(np.sin(theta)), sign=sign,
                g2=as_bf16(g2), g2i=as_bf16(g2i))


def _t5_bucket_np(rel):
    half = NUM_BUCKETS // 2
    max_exact = half // 2
    n = np.abs(rel)
    ret = np.where(rel > 0, half, 0)
    large = max_exact + (np.log(np.maximum(n, 1).astype(np.float32) / np.float32(max_exact))
                         / np.float32(math.log(MAX_DISTANCE / max_exact))
                         * np.float32(half - max_exact)).astype(np.int32)
    large = np.minimum(large, half - 1)
    return ret + np.where(n < max_exact, n, large)


@functools.lru_cache(maxsize=None)
def _bucket_tables():
    qi = np.arange(QBLK)[:, None]
    c = np.arange(2 * QBLK)[None, :]
    rel = c - RADIUS - qi
    tabs = []
    for _, dil in DILATED_GROUPS:
        b = _t5_bucket_np(rel * dil)
        tabs.append(np.where(np.abs(rel) <= RADIUS, b, -1))
    return np.stack(tabs).astype(np.int32)


@functools.lru_cache(maxsize=None)
def _small_consts():
    bands = np.linspace(1e-4, FILTER_BANDS - 1, FILTER_BANDS, dtype=np.float32)[None, :]
    deltas = np.abs(np.linspace(math.log(DECAY_TARGET) / SLOW_DECAY_PCT,
                                math.log(DECAY_TARGET) / FAST_DECAY_PCT, D_HYENA, dtype=np.float32))[None, :]
    head = np.arange(ATTN_WIDTH) // HEAD_DIM
    blockdiag = (head[:, None] == head[None, :]).astype(np.float32) / HEAD_DIM
    return bands, deltas, blockdiag.astype(BF16)


def _mod_kernel(c_ref, w_ref, b_ref, o_ref):
    c = c_ref[...]
    s = c / (1.0 + jnp.exp(-c))
    o_ref[...] = _dot(s, w_ref[...], HIGHEST) + b_ref[...]


def _modulation(c, ada_w, ada_b):
    nb, _ = c.shape
    nw = ada_w.shape[1]
    tn = 1024
    return pl.pallas_call(
        _mod_kernel,
        name="mod",
        grid=(nw // tn,),
        in_specs=[pl.BlockSpec((nb, D_MODEL), lambda j: (0, 0)),
                  pl.BlockSpec((D_MODEL, tn), lambda j: (0, j)),
                  pl.BlockSpec((1, tn), lambda j: (0, j))],
        out_specs=pl.BlockSpec((nb, tn), lambda j: (0, j)),
        out_shape=jax.ShapeDtypeStruct((nb, nw), F32),
        compiler_params=_cparams("parallel"),
    )(c, ada_w, ada_b.reshape(1, nw))


def _bias_kernel(rb_ref, bk_ref, o_ref):
    g = pl.program_id(0)
    bk = bk_ref[0]
    for h in range(HEADS_PER_GROUP):
        acc = jnp.full(bk.shape, NEG_INF, F32)
        for b in range(NUM_BUCKETS):
            acc = jnp.where(bk == b, rb_ref[b, g * HEADS_PER_GROUP + h], acc)
        o_ref[0, h] = acc


def _bias_tables(rel_bias):
    bk = jnp.asarray(_bucket_tables())
    ng = len(DILATED_GROUPS)
    return pl.pallas_call(
        _bias_kernel,
        name="bias_tab",
        grid=(ng,),
        in_specs=[pl.BlockSpec(memory_space=pltpu.SMEM),
                  pl.BlockSpec((1, QBLK, 2 * QBLK), lambda g: (g, 0, 0))],
        out_specs=pl.BlockSpec((1, HEADS_PER_GROUP, QBLK, 2 * QBLK), lambda g: (g, 0, 0, 0)),
        out_shape=jax.ShapeDtypeStruct((ng, HEADS_PER_GROUP, QBLK, 2 * QBLK), F32),
        compiler_params=_cparams("arbitrary"),
    )(rel_bias, bk)


def _filter_kernel(w1t_ref, w1c_ref, w1s_ref, b1_ref, w2_ref, b2_ref, w3_ref, b3_ref, fr_ref, wo_ref,
                   bands_ref, deltas_ref, k_ref, asum_ref, *, seq, rows):
    i = pl.program_id(0)

    def lag(shape, axis):
        m = i * rows + lax.broadcasted_iota(jnp.int32, shape, axis)
        return m, jnp.where(m < seq, m, 2 * seq - m).astype(F32)

    _, pos_l = lag((1, rows), 1)
    t_l = pos_l / float(seq - 1)
    arg = bands_ref[...] * ((2.0 * math.pi) * pos_l / float(seq))
    fr = fr_ref[...]
    z = w1t_ref[...] * t_l + _dot(w1c_ref[...], jnp.cos(arg), HIGHEST) \
        + _dot(w1s_ref[...], -jnp.sin(arg), HIGHEST) + b1_ref[...]
    h = jnp.sin(fr * z)
    h = jnp.sin(fr * (_dot(w2_ref[...], h, HIGHEST) + b2_ref[...]))
    h = jnp.sin(fr * (_dot(w3_ref[...], h, HIGHEST) + b3_ref[...]))
    m, pos = lag((rows, 1), 0)
    t = pos / float(seq - 1)
    k = _dot(h.T, wo_ref[...], HIGHEST) * jnp.exp(-t * deltas_ref[...])
    k = jnp.where(m == seq, 0.0, k)
    k_ref[...] = k

    @pl.when(i == 0)
    def _():
        asum_ref[...] = jnp.zeros_like(asum_ref)

    asum_ref[...] += jnp.sum(jnp.abs(k), axis=0, keepdims=True)


def _filter_taps(seq, w1, b1, w2, b2, w3, b3, freq, w_out):
    bands, deltas, _ = _small_consts()
    rows = 1024
    nblk = 2 * seq // rows
    half = nblk // 2
    hid = FILTER_HIDDEN
    col = lambda a: a.reshape(-1, 1)
    small = lambda shape: pl.BlockSpec(shape, lambda i: (0, 0))
    return pl.pallas_call(
        functools.partial(_filter_kernel, seq=seq, rows=rows),
        name="filter_taps",
        grid=(nblk,),
        in_specs=[small((hid, 1)), small((hid, FILTER_BANDS)), small((hid, FILTER_BANDS)), small((hid, 1)),
                  small((hid, hid)), small((hid, 1)), small((hid, hid)), small((hid, 1)), small((hid, 1)),
                  pl.BlockSpec((hid, D_HYENA), lambda i: (0, jnp.where(i >= half, 1, 0))),
                  small((FILTER_BANDS, 1)), small((1, D_HYENA))],
        out_specs=[pl.BlockSpec((rows, D_HYENA), lambda i: (i, 0)),
                   pl.BlockSpec((1, D_HYENA), lambda i: (0, 0))],
        out_shape=[jax.ShapeDtypeStruct((2 * seq, D_HYENA), F32),
                   jax.ShapeDtypeStruct((1, D_HYENA), F32)],
        compiler_params=_cparams("arbitrary"),
    )(w1[0:1].T, w1[1:1 + FILTER_BANDS].T, w1[1 + FILTER_BANDS:].T, col(b1), w2.T, col(b2), w3.T, col(b3),
      col(freq), w_out, jnp.asarray(bands.T), jnp.asarray(deltas))


def _rotate(re, im, c, s):
    return c * re - s * im, s * re + c * im


def _store_spectrum_block(o_ref, re, im):
    c = re.shape[-1]
    o_ref[0, 0] = re.reshape(DFT_KP, SUBLANES, c)
    o_ref[0, 1] = im.reshape(DFT_KP, SUBLANES, c)


def _dft1_kernel(x_ref, f0_ref, rc_ref, rs_ref, o_ref):
    c = x_ref.shape[-1]
    x = x_ref[0].reshape(DFT_H1 * SUBLANES, c).astype(BF16)
    a = _dot(f0_ref[...], x)
    re, im = _rotate(a[:DFT_ROWS], a[DFT_ROWS:], rc_ref[0], rs_ref[0])
    _store_spectrum_block(o_ref, re, im)


def _dft1_filter_kernel(lo_ref, hi_ref, f0_ref, rc_ref, rs_ref, sign_ref, o_ref):
    c = lo_ref.shape[-1]
    flat = lambda ref: ref[0].reshape(DFT_H1 * SUBLANES, c).astype(BF16)
    lo = _dot(f0_ref[...], flat(lo_ref))
    hi = _dot(f0_ref[...], flat(hi_ref))
    sign = sign_ref[...]
    re, im = _rotate(lo[:DFT_ROWS] + sign * hi[:DFT_ROWS], lo[DFT_ROWS:] + sign * hi[DFT_ROWS:],
                     rc_ref[0], rs_ref[0])
    _store_spectrum_block(o_ref, re, im)


def _dft_stage1(x, consts, is_filter):
    bsz, rows, c = x.shape
    n2n = rows // (DFT_N1 if is_filter else DFT_H1)
    nch = n2n // SUBLANES
    rot = pl.BlockSpec((1, DFT_ROWS, 1), lambda j, b: (j, 0, 0))
    f0 = _const_spec((2 * DFT_ROWS, DFT_H1 * SUBLANES))
    if is_filter:
        xv = x.reshape(2, DFT_H1, n2n, c)
        data = [pl.BlockSpec((1, DFT_H1, SUBLANES, c), lambda j, b: (0, 0, j, 0)),
                pl.BlockSpec((1, DFT_H1, SUBLANES, c), lambda j, b: (1, 0, j, 0))]
        args = (xv, xv, consts["f0"], consts["rot_c"], consts["rot_s"], consts["sign"])
        specs = data + [f0, rot, rot, _const_spec((DFT_ROWS, 1))]
        body = _dft1_filter_kernel
    else:
        xv = x.reshape(bsz, DFT_H1, n2n, c)
        args = (xv, consts["f0"], consts["rot_c"], consts["rot_s"])
        specs = [pl.BlockSpec((1, DFT_H1, SUBLANES, c), lambda j, b: (b, 0, j, 0)), f0, rot, rot]
        body = _dft1_kernel
    return pl.pallas_call(
        body,
        name="dft1",
        grid=(nch, bsz),
        in_specs=specs,
        out_specs=pl.BlockSpec((1, 2, DFT_KP, SUBLANES, c), lambda j, b: (b, 0, 0, j, 0)),
        out_shape=jax.ShapeDtypeStruct((bsz, 2, DFT_KP, n2n, c), F32),
        compiler_params=_cparams("parallel", "parallel"),
    )(*args)


def _filter_spec_kernel(a_ref, g2_ref, asum_ref, o_ref, *, kb, n2n):
    scale = 1.0 / asum_ref[...]
    for j in range(kb):
        a = jnp.concatenate([a_ref[0, 0, j], a_ref[0, 1, j]], axis=0).astype(BF16)
        b = _dot(g2_ref[...], a) * scale
        o_ref[0, j] = b[:n2n]
        o_ref[1, j] = b[n2n:]


def _filter_spectrum(af, g2, asum, kb):
    _, _, kp, n2n, c = af.shape
    return pl.pallas_call(
        functools.partial(_filter_spec_kernel, kb=kb, n2n=n2n),
        name="filter_spec",
        grid=(kp // kb,),
        in_specs=[pl.BlockSpec((1, 2, kb, n2n, c), lambda i: (0, 0, i, 0, 0)),
                  pl.BlockSpec((2 * n2n, 2 * n2n), lambda i: (0, 0)),
                  pl.BlockSpec((1, c), lambda i: (0, 0))],
        out_specs=pl.BlockSpec((2, kb, n2n, c), lambda i: (0, i, 0, 0)),
        out_shape=jax.ShapeDtypeStruct((2, kp, n2n, c), F32),
        compiler_params=_cparams("parallel"),
    )(af, g2, asum)


def _dft2_kernel(a_ref, ks_ref, g2_ref, g2i_ref, o_ref, *, kb, n2n):
    for j in range(kb):
        a = jnp.concatenate([a_ref[0, 0, j], a_ref[0, 1, j]], axis=0).astype(BF16)
        b = _dot(g2_ref[...], a)
        br, bi = b[:n2n], b[n2n:]
        kr, ki = ks_ref[0, j], ks_ref[1, j]
        p = jnp.concatenate([br * kr - bi * ki, br * ki + bi * kr], axis=0).astype(BF16)
        q = _dot(g2i_ref[...], p)
        o_ref[0, 0, j] = q[:n2n]
        o_ref[0, 1, j] = q[n2n:]


def _dft_stage2(a, kspec, g2, g2i, kb):
    bsz, _, kp, n2n, c = a.shape
    return pl.pallas_call(
        functools.partial(_dft2_kernel, kb=kb, n2n=n2n),
        name="dft2",
        grid=(kp // kb, bsz),
        in_specs=[pl.BlockSpec((1, 2, kb, n2n, c), lambda i, b: (b, 0, i, 0, 0)),
                  pl.BlockSpec((2, kb, n2n, c), lambda i, b: (0, i, 0, 0)),
                  pl.BlockSpec((2 * n2n, 2 * n2n), lambda i, b: (0, 0)),
                  pl.BlockSpec((2 * n2n, 2 * n2n), lambda i, b: (0, 0))],
        out_specs=pl.BlockSpec((1, 2, kb, n2n, c), lambda i, b: (b, 0, i, 0, 0)),
        out_shape=jax.ShapeDtypeStruct(a.shape, F32),
        compiler_params=_cparams("parallel", "parallel"),
    )(a, kspec, g2, g2i)


def _dft3_kernel(q_ref, m0_ref, rc_ref, rs_ref, w_ref, x0_ref, d_ref, y_ref):
    c = w_ref.shape[-1]
    qr = q_ref[0, 0].reshape(DFT_ROWS, c)
    qi = q_ref[0, 1].reshape(DFT_ROWS, c)
    re, im = _rotate(qr, qi, rc_ref[0], -rs_ref[0])
    q = jnp.concatenate([re, im], axis=0).astype(BF16)
    conv = _dot(m0_ref[...], q).reshape(DFT_H1, SUBLANES, c)
    y_ref[0] = x0_ref[0] * (conv + w_ref[0] * d_ref[...])


def _dft_stage3(q, consts, w, x0, d):
    bsz, seq, c = w.shape
    n2n = seq // DFT_H1
    nch = n2n // SUBLANES
    view = lambda a: a.reshape(bsz, DFT_H1, n2n, c)
    data = pl.BlockSpec((1, DFT_H1, SUBLANES, c), lambda j, b: (b, 0, j, 0))
    rot = pl.BlockSpec((1, DFT_ROWS, 1), lambda j, b: (j, 0, 0))
    y = pl.pallas_call(
        _dft3_kernel,
        name="dft3",
        grid=(nch, bsz),
        in_specs=[pl.BlockSpec((1, 2, DFT_KP, SUBLANES, c), lambda j, b: (b, 0, 0, j, 0)),
                  _const_spec((DFT_H1 * SUBLANES, 2 * DFT_ROWS)), rot, rot, data, data,
                  _const_spec((1, c))],
        out_specs=data,
        out_shape=jax.ShapeDtypeStruct((bsz, DFT_H1, n2n, c), F32),
        compiler_params=_cparams("parallel", "parallel"),
    )(q, consts["m0"], consts["rot_c"], consts["rot_s"], view(w), view(x0), d)
    return y.reshape(bsz, seq, c)


def _inproj_kernel(x_ref, xp_ref, xn_ref, mod_ref, g1_ref, win_ref, cw_ref, cb_ref, gq_ref, gk_ref, bd_ref,
                   w_out, x0_out, q_out, k_out, v_out, g_out, *, tm, nt):
    i = pl.program_id(1)
    mod = mod_ref[0]
    sh1 = mod[:, 0:D_MODEL]
    scale1 = g1_ref[...] * (1.0 + mod[:, D_MODEL:2 * D_MODEL])

    def norm_mod(x):
        ms = jnp.mean(x * x, axis=-1, keepdims=True)
        return ((x * lax.rsqrt(ms + EPS)) * scale1 + sh1).astype(BF16)

    u = norm_mod(x_ref[0])
    uh = norm_mod(jnp.concatenate([xp_ref[0], xn_ref[0]], axis=0))
    ue = jnp.concatenate([u, uh], axis=0)
    row = lax.broadcasted_iota(jnp.int32, (tm, 1), 0)

    def conv_chunk(c):
        cols = slice(c * D_HYENA, (c + 1) * D_HYENA)
        ze = _dot(ue, win_ref[:, cols])
        z, zh = ze[:tm], ze[tm:]
        prev = jnp.where(i > 0, zh[7:8], 0.0)
        nxt = jnp.where(i < nt - 1, zh[8:9], 0.0)
        zm = jnp.where(row == 0, prev, pltpu.roll(z, 1, 0))
        zp = jnp.where(row == tm - 1, nxt, pltpu.roll(z, tm - 1, 0))
        return zm * cw_ref[0:1, cols] + z * cw_ref[1:2, cols] + zp * cw_ref[2:3, cols] + cb_ref[:, cols]

    x0_out[0] = conv_chunk(0)
    w_out[0] = conv_chunk(1) * conv_chunk(2)

    base = 3 * D_HYENA

    def head_norm(z, gain):
        ms = _dot((z * z).astype(BF16), bd_ref[...])
        return (z * lax.rsqrt(ms + EPS)) * gain

    zq = _dot(u, win_ref[:, base:base + ATTN_WIDTH])
    q_out[0] = head_norm(zq, gq_ref[...]) * (HEAD_DIM ** -0.5)
    zk = _dot(u, win_ref[:, base + ATTN_WIDTH:base + 2 * ATTN_WIDTH])
    k_out[0] = head_norm(zk, gk_ref[...])
    v_out[0] = _dot(u, win_ref[:, base + 2 * ATTN_WIDTH:base + 3 * ATTN_WIDTH])
    zg = _dot(u, win_ref[:, base + 3 * ATTN_WIDTH:])
    g_out[0] = 1.0 / (1.0 + jnp.exp(-zg))


def _in_projection(x, mod, norm1_g, w_in_bf16, conv_w, conv_b, q_norm_g, k_norm_g, tm):
    bsz, seq, _ = x.shape
    nt = seq // tm
    hb = tm // SUBLANES
    _, _, blockdiag = _small_consts()
    tile = lambda w: pl.BlockSpec((1, tm, w), lambda b, i: (b, i, 0))
    outs = [D_HYENA, D_HYENA, ATTN_WIDTH, ATTN_WIDTH, ATTN_WIDTH, 2 * D_MODEL]
    return pl.pallas_call(
        functools.partial(_inproj_kernel, tm=tm, nt=nt),
        name="inproj",
        grid=(bsz, nt),
        in_specs=[tile(D_MODEL),
                  pl.BlockSpec((1, SUBLANES, D_MODEL), lambda b, i: (b, jnp.maximum(i * hb - 1, 0), 0)),
                  pl.BlockSpec((1, SUBLANES, D_MODEL),
                               lambda b, i: (b, jnp.minimum((i + 1) * hb, seq // SUBLANES - 1), 0)),
                  pl.BlockSpec((1, 1, 6 * D_MODEL), lambda b, i: (b, 0, 0)),
                  _const_spec((1, D_MODEL)),
                  _const_spec((D_MODEL, IN_PROJ_WIDTH)),
                  _const_spec((3, 3 * D_HYENA)),
                  _const_spec((1, 3 * D_HYENA)),
                  _const_spec((1, ATTN_WIDTH)),
                  _const_spec((1, ATTN_WIDTH)),
                  _const_spec((ATTN_WIDTH, ATTN_WIDTH))],
        out_specs=[tile(w) for w in outs],
        out_shape=[jax.ShapeDtypeStruct((bsz, seq, w), F32) for w in outs],
        compiler_params=_cparams("parallel", "parallel"),
    )(x, x, x, mod, norm1_g.reshape(1, -1), w_in_bf16, conv_w, conv_b.reshape(1, -1),
      q_norm_g.reshape(1, -1), k_norm_g.reshape(1, -1), jnp.asarray(blockdiag))


def _attn_kernel(*refs, tq, sub, dil):
    npair = PAIRS_PER_GROUP
    q_refs = refs[0:npair]
    k_refs = refs[npair:4 * npair]
    v_refs = refs[4 * npair:7 * npair]
    tb_ref = refs[7 * npair]
    o_refs = refs[7 * npair + 1:8 * npair + 1]
    l_refs = refs[8 * npair + 1:9 * npair + 1]
    t0 = pl.program_id(1) * tq
    lower = lax.broadcasted_iota(jnp.int32, (1, PAIR_WIDTH), 1) < HEAD_DIM

    def rows(r, count, offset=0):
        if dil == 1:
            return pl.ds(offset, count)
        return pl.ds(r + offset * dil, count, stride=dil)

    def residue(r):
        for pair in range(npair):
            kp, kc, kn = k_refs[3 * pair:3 * pair + 3]
            vp, vc, vn = v_refs[3 * pair:3 * pair + 3]
            kfull = jnp.concatenate([kp[0, rows(r, RADIUS), :], kc[0, rows(r, tq), :], kn[0, rows(r, RADIUS), :]],
                                    axis=0).astype(BF16)
            vfull = jnp.concatenate([vp[0, rows(r, RADIUS), :], vc[0, rows(r, tq), :], vn[0, rows(r, RADIUS), :]],
                                    axis=0).astype(BF16)
            for j in range(tq // QBLK):
                kpos = t0 - RADIUS + QBLK * j + lax.broadcasted_iota(jnp.int32, (1, 2 * QBLK), 1)
                valid = (kpos >= 0) & (kpos < sub)
                win = slice(QBLK * j, QBLK * (j + 2))
                q2 = q_refs[pair][0, rows(r, QBLK, QBLK * j), :]
                k2 = kfull[win]
                v2 = vfull[win]
                res = []
                for hh in range(2):
                    sel = lower if hh == 0 else jnp.logical_not(lower)
                    qm = jnp.where(sel, q2, 0.0).astype(BF16)
                    s = lax.dot_general(qm, k2, (((1,), (1,)), ((), ())), preferred_element_type=F32)
                    s = jnp.where(valid, s + tb_ref[0, 2 * pair + hh], NEG_INF)
                    m = jnp.max(s, axis=-1, keepdims=True)
                    p = jnp.exp(s - m)
                    den = jnp.sum(p, axis=-1, keepdims=True)
                    res.append((_dot(p.astype(BF16), v2) / den, m + jnp.log(den)))
                o_refs[pair][0, rows(r, QBLK, QBLK * j), :] = jnp.where(lower, res[0][0], res[1][0])
                l_refs[pair][0, rows(r, QBLK, QBLK * j), :] = jnp.where(lower, res[0][1], res[1][1])

    if dil == 1:
        residue(0)
    else:
        def body(r, carry):
            residue(r)
            return carry
        lax.fori_loop(0, dil, body, 0, unroll=2)


def _dilated_group(q, k, v, tb, gi, dil, tq):
    bsz, seq, _ = q.shape
    sub = seq // dil
    tq = min(tq, sub)
    tp = tq * dil
    halo = RADIUS * dil
    hb = tp // halo
    lane0 = gi * PAIRS_PER_GROUP
    cur = lambda p: pl.BlockSpec((1, tp, PAIR_WIDTH), lambda b, i: (b, i, lane0 + p))
    prev = lambda p: pl.BlockSpec((1, halo, PAIR_WIDTH), lambda b, i: (b, jnp.maximum(i * hb - 1, 0), lane0 + p))
    nxt = lambda p: pl.BlockSpec((1, halo, PAIR_WIDTH),
                                 lambda b, i: (b, jnp.minimum((i + 1) * hb, seq // halo - 1), lane0 + p))
    out = pl.BlockSpec((1, tp, PAIR_WIDTH), lambda b, i: (b, i, 0))
    pairs = range(PAIRS_PER_GROUP)
    halo_specs = [s(p) for p in pairs for s in (prev, cur, nxt)]
    res = pl.pallas_call(
        functools.partial(_attn_kernel, tq=tq, sub=sub, dil=dil),
        name="attn",
        grid=(bsz, seq // tp),
        in_specs=[cur(p) for p in pairs] + halo_specs + halo_specs
                 + [pl.BlockSpec((1, HEADS_PER_GROUP, QBLK, 2 * QBLK), lambda b, i: (gi, 0, 0, 0))],
        out_specs=[out] * (2 * PAIRS_PER_GROUP),
        out_shape=[jax.ShapeDtypeStruct((bsz, seq, PAIR_WIDTH), F32)] * (2 * PAIRS_PER_GROUP),
        compiler_params=_cparams("parallel", "parallel"),
    )(*([q] * PAIRS_PER_GROUP + [k] * (3 * PAIRS_PER_GROUP) + [v] * (3 * PAIRS_PER_GROUP) + [tb]))
    return res[:PAIRS_PER_GROUP], res[PAIRS_PER_GROUP:]


def _merge_kernel(*refs):
    ng = len(DILATED_GROUPS)
    npair = PAIRS_PER_GROUP
    x_ref, yhy_ref = refs[0:2]
    o_refs = refs[2:2 + ng * npair]
    l_refs = refs[2 + ng * npair:2 + 2 * ng * npair]
    g_ref, mod_ref, g2_ref, whb_ref, wab_ref, wo_ref, wup_ref, wdn_ref, out_ref = refs[2 + 2 * ng * npair:]
    mod = mod_ref[0]
    gt1 = mod[:, 2 * D_MODEL:3 * D_MODEL]
    sh2 = mod[:, 3 * D_MODEL:4 * D_MODEL]
    sc2 = mod[:, 4 * D_MODEL:5 * D_MODEL]
    gt2 = mod[:, 5 * D_MODEL:6 * D_MODEL]

    merged = []
    for p in range(npair):
        lses = [l_refs[g * npair + p][0] for g in range(ng)]
        lm = functools.reduce(jnp.maximum, lses)
        es = [jnp.exp(l - lm) for l in lses]
        num = sum(e * o_refs[g * npair + p][0] for g, e in enumerate(es))
        merged.append(num / sum(es))
    y_at = jnp.concatenate(merged, axis=-1)

    g = g_ref[0]
    mix = g[:, :D_MODEL] * _dot(yhy_ref[0].astype(BF16), whb_ref[...]) \
        + g[:, D_MODEL:] * _dot(y_at.astype(BF16), wab_ref[...])
    h = x_ref[0] + gt1 * _dot(mix.astype(BF16), wo_ref[...])

    ms = jnp.mean(h * h, axis=-1, keepdims=True)
    u2 = (h * lax.rsqrt(ms + EPS)) * (g2_ref[...] * (1.0 + sc2)) + sh2
    up = jnp.maximum(_dot(u2.astype(BF16), wup_ref[...]), 0.0)
    ff = _dot((up * up).astype(BF16), wdn_ref[...])
    out_ref[0] = h + gt2 * ff


def _merge_mlp(x, y_hy, outs, lses, gates, mod, norm2_g, w_hy_br, w_at_br, w_out, w_up, w_down, tm):
    bsz, seq, _ = x.shape
    tile = lambda w: pl.BlockSpec((1, tm, w), lambda b, i: (b, i, 0))
    n_attn = len(outs) + len(lses)
    return pl.pallas_call(
        _merge_kernel,
        name="merge_mlp",
        grid=(bsz, seq // tm),
        in_specs=[tile(D_MODEL), tile(D_HYENA)] + [tile(PAIR_WIDTH)] * n_attn + [tile(2 * D_MODEL),
                  pl.BlockSpec((1, 1, 6 * D_MODEL), lambda b, i: (b, 0, 0)),
                  _const_spec((1, D_MODEL)),
                  _const_spec((D_HYENA, D_MODEL)),
                  _const_spec((PAIRS_PER_GROUP * PAIR_WIDTH, D_MODEL)),
                  _const_spec((D_MODEL, D_MODEL)),
                  _const_spec((D_MODEL, D_FF)),
                  _const_spec((D_FF, D_MODEL))],
        out_specs=tile(D_MODEL),
        out_shape=jax.ShapeDtypeStruct(x.shape, F32),
        compiler_params=_cparams("parallel", "parallel"),
    )(x, y_hy, *outs, *lses, gates, mod, norm2_g.reshape(1, -1), w_hy_br, w_at_br, w_out, w_up, w_down)


ATTN_TQ = {1: 512, 4: 256, 16: 128}


def _long_conv_gate(w, x0, kspec, hyena_d, consts):
    a = _dft_stage1(w, consts, is_filter=False)
    qf = _dft_stage2(a, kspec, consts["g2"], consts["g2i"], 5)
    return _dft_stage3(qf, consts, w, x0, hyena_d.reshape(1, -1))


def _filter_spec(seq, consts, filt):
    taps, asum = _filter_taps(seq, *filt)
    af = _dft_stage1(taps[None], consts, is_filter=True)
    return _filter_spectrum(af, consts["g2"], asum, 5)


def _encoder_layer(x, mod, tb, p):
    bsz, seq, _ = x.shape
    consts = {k: jnp.asarray(v) for k, v in _dft_consts(seq).items()}
    mod3 = mod.reshape(bsz, 1, 6 * D_MODEL)
    w, x0, q, k, v, gates = _in_projection(x, mod3, p["norm1_g"], p["w_in"], p["conv_w"], p["conv_b"],
                                           p["q_norm_g"], p["k_norm_g"], tm=256)
    kspec = _filter_spec(seq, consts, p["filt"])
    y_hy = _long_conv_gate(w, x0, kspec, p["hyena_d"], consts)
    outs, lses = [], []
    for gi, (_, dil) in enumerate(DILATED_GROUPS):
        o, lse = _dilated_group(q, k, v, tb, gi, dil, ATTN_TQ[dil])
        outs.extend(o)
        lses.extend(lse)
    return _merge_mlp(x, y_hy, outs, lses, gates, mod3, p["norm2_g"], p["w_hy_br"], p["w_at_br"], p["w_out"],
                      p["w_up"], p["w_down"], tm=256)


def kernel(x_prompt, x_sample, c_prompt, c_sample, rel_bias, ada_w, ada_b, norm1_g, w_in, conv_w, conv_b,
           filt_w1, filt_b1, filt_w2, filt_b2, filt_w3, filt_b3, filt_freq, filt_w_out, hyena_d, q_norm_g,
           k_norm_g, w_hy_br, w_at_br, w_out, norm2_g, w_up, w_down):
    depth = ada_w.shape[0]
    tb = _bias_tables(rel_bias)
    y_prompt, y_sample = x_prompt, x_sample
    nbp = c_prompt.shape[0]
    for l in range(depth):
        p = dict(norm1_g=norm1_g[l], w_in=w_in[l].astype(BF16), conv_w=conv_w[l], conv_b=conv_b[l],
                 filt=(filt_w1[l], filt_b1[l], filt_w2[l], filt_b2[l], filt_w3[l], filt_b3[l], filt_freq[l],
                       filt_w_out[l]),
                 hyena_d=hyena_d[l], q_norm_g=q_norm_g[l], k_norm_g=k_norm_g[l],
                 w_hy_br=w_hy_br[l].astype(BF16), w_at_br=w_at_br[l].astype(BF16), w_out=w_out[l].astype(BF16),
                 norm2_g=norm2_g[l], w_up=w_up[l].astype(BF16), w_down=w_down[l].astype(BF16))
        mod = _modulation(jnp.concatenate([c_prompt, c_sample], axis=0), ada_w[l], ada_b[l])
        y_prompt = _encoder_layer(y_prompt, mod[:nbp], tb, p)
        y_sample = _encoder_layer(y_sample, mod[nbp:], tb, p)
    return (y_prompt, y_sample)
```

```python
import functools
import math

import numpy as np
import jax
import jax.numpy as jnp
from jax import lax
from jax.experimental import pallas as pl
from jax.experimental.pallas import tpu as pltpu

F32 = jnp.float32
BF16 = jnp.bfloat16
HIGHEST = lax.Precision.HIGHEST

D_MODEL = 1024
EPS = 1e-6
HEAD_DIM = 64
N_HEADS = 12
DILATED_GROUPS = ((128, 1), (512, 4), (2048, 16))
HEADS_PER_GROUP = 4
ATTN_WIDTH = 768
PAIR_WIDTH = 2 * HEAD_DIM
PAIRS_PER_GROUP = HEADS_PER_GROUP // 2
RADIUS = 64
NUM_BUCKETS = 32
MAX_DISTANCE = 1024
NEG_INF = -1e30
D_HYENA = 768
FILTER_BANDS = 16
FILTER_HIDDEN = 64
DECAY_TARGET = 1e-2
FAST_DECAY_PCT = 0.3
SLOW_DECAY_PCT = 1.5
D_FF = 4096
IN_PROJ_WIDTH = 6656

SUBLANES = 8
DFT_N1 = 128
DFT_H1 = 64
DFT_KP = 65
DFT_ROWS = DFT_KP * SUBLANES
QBLK = 128
VMEM_LIMIT = 56 * 1024 * 1024


def _cparams(*sem):
    return pltpu.CompilerParams(dimension_semantics=sem, vmem_limit_bytes=VMEM_LIMIT)


def _const_spec(shape):
    nd = len(shape)
    return pl.BlockSpec(shape, lambda *_: (0,) * nd, pipeline_mode=pl.Buffered(1))


def _dot(a, b, precision=None):
    return jnp.dot(a, b, preferred_element_type=F32, precision=precision)


@functools.lru_cache(maxsize=None)
def _dft_consts(L):
    n2n = L // DFT_H1
    n = 2 * L
    nch = n2n // SUBLANES
    r = np.arange(SUBLANES)
    eye = np.eye(SUBLANES)
    k1 = np.arange(DFT_KP)
    n1 = np.arange(DFT_H1)
    ang = -2.0 * np.pi * (r[:, None, None] * k1[None, :, None] / n + k1[None, :, None] * n1[None, None, :] / DFT_N1)
    f1 = np.stack([np.cos(ang), np.sin(ang)], axis=1)
    f0 = np.einsum("rpkn,rs->pkrns", f1, eye).reshape(2 * DFT_ROWS, DFT_H1 * SUBLANES)
    angc = 2.0 * np.pi * (n1[None, :, None] * k1[None, None, :] / DFT_N1 + r[:, None, None] * k1[None, None, :] / n)
    mult = np.full(DFT_KP, 2.0)
    mult[0] = mult[-1] = 1.0
    mi = np.stack([mult * np.cos(angc), -mult * np.sin(angc)], axis=2) / n
    m0 = np.einsum("rnpk,rs->nrpks", mi, eye).reshape(DFT_H1 * SUBLANES, 2 * DFT_ROWS)
    theta = -2.0 * np.pi * SUBLANES * np.arange(nch)[:, None] * k1[None, :] / n
    rep = lambda a: np.repeat(a, SUBLANES, axis=1)[:, :, None].astype(np.float32)
    sign = np.repeat(np.where(k1 % 2 == 0, 1.0, -1.0), SUBLANES)[:, None].astype(np.float32)
    a2 = -2.0 * np.pi * np.arange(n2n)[:, None] * np.arange(n2n)[None, :] / n2n
    fr, fi = np.cos(a2), np.sin(a2)
    g2 = np.block([[fr, -fi], [fi, fr]])
    g2i = np.block([[fr, fi], [-fi, fr]])
    as_bf16 = lambda a: np.asarray(a, np.float32).astype(BF16)
    return dict(f0=as_bf16(f0), m0=as_bf16(m0), rot_c=rep(np.cos(theta)), rot_s=rep(np.sin(theta)), sign=sign,
                g2=as_bf16(g2), g2i=as_bf16(g2i))


def _t5_bucket_np(rel):
    half = NUM_BUCKETS // 2
    max_exact = half // 2
    n = np.abs(rel)
    ret = np.where(rel > 0, half, 0)
    large = max_exact + (np.log(np.maximum(n, 1).astype(np.float32) / np.float32(max_exact))
                         / np.float32(math.log(MAX_DISTANCE / max_exact))
                         * np.float32(half - max_exact)).astype(np.int32)
    large = np.minimum(large, half - 1)
    return ret + np.where(n < max_exact, n, large)


@functools.lru_cache(maxsize=None)
def _bucket_tables():
    qi = np.arange(QBLK)[:, None]
    c = np.arange(2 * QBLK)[None, :]
    rel = c - RADIUS - qi
    tabs = []
    for _, dil in DILATED_GROUPS:
        b = _t5_bucket_np(rel * dil)
        tabs.append(np.where(np.abs(rel) <= RADIUS, b, -1))
    return np.stack(tabs).astype(np.int32)


@functools.lru_cache(maxsize=None)
def _small_consts():
    bands = np.linspace(1e-4, FILTER_BANDS - 1, FILTER_BANDS, dtype=np.float32)[None, :]
    deltas = np.abs(np.linspace(math.log(DECAY_TARGET) / SLOW_DECAY_PCT,
                                math.log(DECAY_TARGET) / FAST_DECAY_PCT, D_HYENA, dtype=np.float32))[None, :]
    head = np.arange(ATTN_WIDTH) // HEAD_DIM
    blockdiag = (head[:, None] == head[None, :]).astype(np.float32) / HEAD_DIM
    return bands, deltas, blockdiag.astype(BF16)


def _mod_kernel(c_ref, w_ref, b_ref, o_ref):
    c = c_ref[...]
    s = c / (1.0 + jnp.exp(-c))
    o_ref[...] = _dot(s, w_ref[...], HIGHEST) + b_ref[...]


def _modulation(c, ada_w, ada_b):
    nb, _ = c.shape
    nw = ada_w.shape[1]
    tn = 1024
    return pl.pallas_call(
        _mod_kernel,
        name="mod",
        grid=(nw // tn,),
        in_specs=[pl.BlockSpec((nb, D_MODEL), lambda j: (0, 0)),
                  pl.BlockSpec((D_MODEL, tn), lambda j: (0, j)),
                  pl.BlockSpec((1, tn), lambda j: (0, j))],
        out_specs=pl.BlockSpec((nb, tn), lambda j: (0, j)),
        out_shape=jax.ShapeDtypeStruct((nb, nw), F32),
        compiler_params=_cparams("parallel"),
    )(c, ada_w, ada_b.reshape(1, nw))


def _bias_kernel(rb_ref, bk_ref, o_ref):
    g = pl.program_id(0)
    bk = bk_ref[0]
    for h in range(HEADS_PER_GROUP):
        acc = jnp.full(bk.shape, NEG_INF, F32)
        for b in range(NUM_BUCKETS):
            acc = jnp.where(bk == b, rb_ref[b, g * HEADS_PER_GROUP + h], acc)
        o_ref[0, h] = acc


def _bias_tables(rel_bias):
    bk = jnp.asarray(_bucket_tables())
    ng = len(DILATED_GROUPS)
    return pl.pallas_call(
        _bias_kernel,
        name="bias_tab",
        grid=(ng,),
        in_specs=[pl.BlockSpec(memory_space=pltpu.SMEM),
                  pl.BlockSpec((1, QBLK, 2 * QBLK), lambda g: (g, 0, 0))],
        out_specs=pl.BlockSpec((1, HEADS_PER_GROUP, QBLK, 2 * QBLK), lambda g: (g, 0, 0, 0)),
        out_shape=jax.ShapeDtypeStruct((ng, HEADS_PER_GROUP, QBLK, 2 * QBLK), F32),
        compiler_params=_cparams("arbitrary"),
    )(rel_bias, bk)


def _filter_kernel(w1t_ref, w1c_ref, w1s_ref, b1_ref, w2_ref, b2_ref, w3_ref, b3_ref, fr_ref, wo_ref,
                   bands_ref, deltas_ref, k_ref, asum_ref, *, seq, rows):
    i = pl.program_id(0)

    def lag(shape, axis):
        m = i * rows + lax.broadcasted_iota(jnp.int32, shape, axis)
        return m, jnp.where(m < seq, m, 2 * seq - m).astype(F32)

    _, pos_l = lag((1, rows), 1)
    t_l = pos_l / float(seq - 1)
    arg = bands_ref[...] * ((2.0 * math.pi) * pos_l / float(seq))
    fr = fr_ref[...]
    z = w1t_ref[...] * t_l + _dot(w1c_ref[...], jnp.cos(arg), HIGHEST) \
        + _dot(w1s_ref[...], -jnp.sin(arg), HIGHEST) + b1_ref[...]
    h = jnp.sin(fr * z)
    h = jnp.sin(fr * (_dot(w2_ref[...], h, HIGHEST) + b2_ref[...]))
    h = jnp.sin(fr * (_dot(w3_ref[...], h, HIGHEST) + b3_ref[...]))
    m, pos = lag((rows, 1), 0)
    t = pos / float(seq - 1)
    k = _dot(h.T, wo_ref[...], HIGHEST) * jnp.exp(-t * deltas_ref[...])
    k = jnp.where(m == seq, 0.0, k)
    k_ref[...] = k

    @pl.when(i == 0)
    def _():
        asum_ref[...] = jnp.zeros_like(asum_ref)

    asum_ref[...] += jnp.sum(jnp.abs(k), axis=0, keepdims=True)


def _filter_taps(seq, w1, b1, w2, b2, w3, b3, freq, w_out):
    bands, deltas, _ = _small_consts()
    rows = 1024
    nblk = 2 * seq // rows
    half = nblk // 2
    hid = FILTER_HIDDEN
    col = lambda a: a.reshape(-1, 1)
    small = lambda shape: pl.BlockSpec(shape, lambda i: (0, 0))
    return pl.pallas_call(
        functools.partial(_filter_kernel, seq=seq, rows=rows),
        name="filter_taps",
        grid=(nblk,),
        in_specs=[small((hid, 1)), small((hid, FILTER_BANDS)), small((hid, FILTER_BANDS)), small((hid, 1)),
                  small((hid, hid)), small((hid, 1)), small((hid, hid)), small((hid, 1)), small((hid, 1)),
                  pl.BlockSpec((hid, D_HYENA), lambda i: (0, jnp.where(i >= half, 1, 0))),
                  small((FILTER_BANDS, 1)), small((1, D_HYENA))],
        out_specs=[pl.BlockSpec((rows, D_HYENA), lambda i: (i, 0)),
                   pl.BlockSpec((1, D_HYENA), lambda i: (0, 0))],
        out_shape=[jax.ShapeDtypeStruct((2 * seq, D_HYENA), F32),
                   jax.ShapeDtypeStruct((1, D_HYENA), F32)],
        compiler_params=_cparams("arbitrary"),
    )(w1[0:1].T, w1[1:1 + FILTER_BANDS].T, w1[1 + FILTER_BANDS:].T, col(b1), w2.T, col(b2), w3.T, col(b3),
      col(freq), w_out, jnp.asarray(bands.T), jnp.asarray(deltas))


def _rotate(re, im, c, s):
    return c * re - s * im, s * re + c * im


def _store_spectrum_block(o_ref, re, im):
    c = re.shape[-1]
    o_ref[0, 0] = re.reshape(DFT_KP, SUBLANES, c)
    o_ref[0, 1] = im.reshape(DFT_KP, SUBLANES, c)


def _dft1_kernel(x_ref, f0_ref, rc_ref, rs_ref, o_ref):
    c = x_ref.shape[-1]
    x = x_ref[0].reshape(DFT_H1 * SUBLANES, c).astype(BF16)
    a = _dot(f0_ref[...], x)
    re, im = _rotate(a[:DFT_ROWS], a[DFT_ROWS:], rc_ref[0], rs_ref[0])
    _store_spectrum_block(o_ref, re, im)


def _dft1_filter_kernel(lo_ref, hi_ref, f0_ref, rc_ref, rs_ref, sign_ref, o_ref):
    c = lo_ref.shape[-1]
    flat = lambda ref: ref[0].reshape(DFT_H1 * SUBLANES, c).astype(BF16)
    lo = _dot(f0_ref[...], flat(lo_ref))
    hi = _dot(f0_ref[...], flat(hi_ref))
    sign = sign_ref[...]
    re, im = _rotate(lo[:DFT_ROWS] + sign * hi[:DFT_ROWS], lo[DFT_ROWS:] + sign * hi[DFT_ROWS:],
                     rc_ref[0], rs_ref[0])
    _store_spectrum_block(o_ref, re, im)


def _dft_stage1(x, consts, is_filter):
    bsz, rows, c = x.shape
    n2n = rows // (DFT_N1 if is_filter else DFT_H1)
    nch = n2n // SUBLANES
    rot = pl.BlockSpec((1, DFT_ROWS, 1), lambda j, b: (j, 0, 0))
    f0 = _const_spec((2 * DFT_ROWS, DFT_H1 * SUBLANES))
    if is_filter:
        xv = x.reshape(2, DFT_H1, n2n, c)
        data = [pl.BlockSpec((1, DFT_H1, SUBLANES, c), lambda j, b: (0, 0, j, 0)),
                pl.BlockSpec((1, DFT_H1, SUBLANES, c), lambda j, b: (1, 0, j, 0))]
        args = (xv, xv, consts["f0"], consts["rot_c"], consts["rot_s"], consts["sign"])
        specs = data + [f0, rot, rot, _const_spec((DFT_ROWS, 1))]
        body = _dft1_filter_kernel
    else:
        xv = x.reshape(bsz, DFT_H1, n2n, c)
        args = (xv, consts["f0"], consts["rot_c"], consts["rot_s"])
        specs = [pl.BlockSpec((1, DFT_H1, SUBLANES, c), lambda j, b: (b, 0, j, 0)), f0, rot, rot]
        body = _dft1_kernel
    return pl.pallas_call(
        body,
        name="dft1",
        grid=(nch, bsz),
        in_specs=specs,
        out_specs=pl.BlockSpec((1, 2, DFT_KP, SUBLANES, c), lambda j, b: (b, 0, 0, j, 0)),
        out_shape=jax.ShapeDtypeStruct((bsz, 2, DFT_KP, n2n, c), F32),
        compiler_params=_cparams("parallel", "parallel"),
    )(*args)


def _filter_spec_kernel(a_ref, g2_ref, asum_ref, o_ref, *, kb, n2n):
    scale = 1.0 / asum_ref[...]
    for j in range(kb):
        a = jnp.concatenate([a_ref[0, 0, j], a_ref[0, 1, j]], axis=0).astype(BF16)
        b = _dot(g2_ref[...], a) * scale
        o_ref[0, j] = b[:n2n]
        o_ref[1, j] = b[n2n:]


def _filter_spectrum(af, g2, asum, kb):
    _, _, kp, n2n, c = af.shape
    return pl.pallas_call(
        functools.partial(_filter_spec_kernel, kb=kb, n2n=n2n),
        name="filter_spec",
        grid=(kp // kb,),
        in_specs=[pl.BlockSpec((1, 2, kb, n2n, c), lambda i: (0, 0, i, 0, 0)),
                  pl.BlockSpec((2 * n2n, 2 * n2n), lambda i: (0, 0)),
                  pl.BlockSpec((1, c), lambda i: (0, 0))],
        out_specs=pl.BlockSpec((2, kb, n2n, c), lambda i: (0, i, 0, 0)),
        out_shape=jax.ShapeDtypeStruct((2, kp, n2n, c), F32),
        compiler_params=_cparams("parallel"),
    )(af, g2, asum)


def _dft2_kernel(a_ref, ks_ref, g2_ref, g2i_ref, o_ref, *, kb, n2n):
    for j in range(kb):
        a = jnp.concatenate([a_ref[0, 0, j], a_ref[0, 1, j]], axis=0).astype(BF16)
        b = _dot(g2_ref[...], a)
        br, bi = b[:n2n], b[n2n:]
        kr, ki = ks_ref[0, j], ks_ref[1, j]
        p = jnp.concatenate([br * kr - bi * ki, br * ki + bi * kr], axis=0).astype(BF16)
        q = _dot(g2i_ref[...], p)
        o_ref[0, 0, j] = q[:n2n]
        o_ref[0, 1, j] = q[n2n:]


def _dft_stage2(a, kspec, g2, g2i, kb):
    bsz, _, kp, n2n, c = a.shape
    return pl.pallas_call(
        functools.partial(_dft2_kernel, kb=kb, n2n=n2n),
        name="dft2",
        grid=(kp // kb, bsz),
        in_specs=[pl.BlockSpec((1, 2, kb, n2n, c), lambda i, b: (b, 0, i, 0, 0)),
                  pl.BlockSpec((2, kb, n2n, c), lambda i, b: (0, i, 0, 0)),
                  pl.BlockSpec((2 * n2n, 2 * n2n), lambda i, b: (0, 0)),
                  pl.BlockSpec((2 * n2n, 2 * n2n), lambda i, b: (0, 0))],
        out_specs=pl.BlockSpec((1, 2, kb, n2n, c), lambda i, b: (b, 0, i, 0, 0)),
        out_shape=jax.ShapeDtypeStruct(a.shape, F32),
        compiler_params=_cparams("parallel", "parallel"),
    )(a, kspec, g2, g2i)


def _dft3_kernel(q_ref, m0_ref, rc_ref, rs_ref, y_ref):
    c = y_ref.shape[-1]
    qr = q_ref[0, 0].reshape(DFT_ROWS, c)
    qi = q_ref[0, 1].reshape(DFT_ROWS, c)
    re, im = _rotate(qr, qi, rc_ref[0], -rs_ref[0])
    q = jnp.concatenate([re, im], axis=0).astype(BF16)
    y_ref[0] = _dot(m0_ref[...], q).reshape(DFT_H1, SUBLANES, c)


def _dft_stage3(q, consts):
    bsz, _, _, n2n, c = q.shape
    nch = n2n // SUBLANES
    rot = pl.BlockSpec((1, DFT_ROWS, 1), lambda j, b: (j, 0, 0))
    y = pl.pallas_call(
        _dft3_kernel,
        name="dft3",
        grid=(nch, bsz),
        in_specs=[pl.BlockSpec((1, 2, DFT_KP, SUBLANES, c), lambda j, b: (b, 0, 0, j, 0)),
                  _const_spec((DFT_H1 * SUBLANES, 2 * DFT_ROWS)), rot, rot],
        out_specs=pl.BlockSpec((1, DFT_H1, SUBLANES, c), lambda j, b: (b, 0, j, 0)),
        out_shape=jax.ShapeDtypeStruct((bsz, DFT_H1, n2n, c), F32),
        compiler_params=_cparams("parallel", "parallel"),
    )(q, consts["m0"], consts["rot_c"], consts["rot_s"])
    return y.reshape(bsz, DFT_H1 * n2n, c)


def _inproj_kernel(x_ref, xp_ref, xn_ref, mod_ref, g1_ref, win_ref, cw_ref, cb_ref, gq_ref, gk_ref, bd_ref,
                   w_out, x0_out, q_out, k_out, v_out, g_out, *, tm, nt):
    i = pl.program_id(1)
    mod = mod_ref[0]
    sh1 = mod[:, 0:D_MODEL]
    scale1 = g1_ref[...] * (1.0 + mod[:, D_MODEL:2 * D_MODEL])

    def norm_mod(x):
        ms = jnp.mean(x * x, axis=-1, keepdims=True)
        return ((x * lax.rsqrt(ms + EPS)) * scale1 + sh1).astype(BF16)

    u = norm_mod(x_ref[0])
    uh = norm_mod(jnp.concatenate([xp_ref[0], xn_ref[0]], axis=0))
    ue = jnp.concatenate([u, uh], axis=0)
    row = lax.broadcasted_iota(jnp.int32, (tm, 1), 0)

    def conv_chunk(c):
        cols = slice(c * D_HYENA, (c + 1) * D_HYENA)
        ze = _dot(ue, win_ref[:, cols])
        z, zh = ze[:tm], ze[tm:]
        prev = jnp.where(i > 0, zh[7:8], 0.0)
        nxt = jnp.where(i < nt - 1, zh[8:9], 0.0)
        zm = jnp.where(row == 0, prev, pltpu.roll(z, 1, 0))
        zp = jnp.where(row == tm - 1, nxt, pltpu.roll(z, tm - 1, 0))
        return zm * cw_ref[0:1, cols] + z * cw_ref[1:2, cols] + zp * cw_ref[2:3, cols] + cb_ref[:, cols]

    x0_out[0] = conv_chunk(0)
    w_out[0] = conv_chunk(1) * conv_chunk(2)

    base = 3 * D_HYENA

    def head_norm(z, gain):
        ms = _dot((z * z).astype(BF16), bd_ref[...])
        return (z * lax.rsqrt(ms + EPS)) * gain

    zq = _dot(u, win_ref[:, base:base + ATTN_WIDTH])
    q_out[0] = head_norm(zq, gq_ref[...]) * (HEAD_DIM ** -0.5)
    zk = _dot(u, win_ref[:, base + ATTN_WIDTH:base + 2 * ATTN_WIDTH])
    k_out[0] = head_norm(zk, gk_ref[...])
    v_out[0] = _dot(u, win_ref[:, base + 2 * ATTN_WIDTH:base + 3 * ATTN_WIDTH])
    zg = _dot(u, win_ref[:, base + 3 * ATTN_WIDTH:])
    g_out[0] = 1.0 / (1.0 + jnp.exp(-zg))


def _in_projection(x, mod, norm1_g, w_in_bf16, conv_w, conv_b, q_norm_g, k_norm_g, tm):
    bsz, seq, _ = x.shape
    nt = seq // tm
    hb = tm // SUBLANES
    _, _, blockdiag = _small_consts()
    tile = lambda w: pl.BlockSpec((1, tm, w), lambda b, i: (b, i, 0))
    outs = [D_HYENA, D_HYENA, ATTN_WIDTH, ATTN_WIDTH, ATTN_WIDTH, 2 * D_MODEL]
    return pl.pallas_call(
        functools.partial(_inproj_kernel, tm=tm, nt=nt),
        name="inproj",
        grid=(bsz, nt),
        in_specs=[tile(D_MODEL),
                  pl.BlockSpec((1, SUBLANES, D_MODEL), lambda b, i: (b, jnp.maximum(i * hb - 1, 0), 0)),
                  pl.BlockSpec((1, SUBLANES, D_MODEL),
                               lambda b, i: (b, jnp.minimum((i + 1) * hb, seq // SUBLANES - 1), 0)),
                  pl.BlockSpec((1, 1, 6 * D_MODEL), lambda b, i: (b, 0, 0)),
                  _const_spec((1, D_MODEL)),
                  _const_spec((D_MODEL, IN_PROJ_WIDTH)),
                  _const_spec((3, 3 * D_HYENA)),
                  _const_spec((1, 3 * D_HYENA)),
                  _const_spec((1, ATTN_WIDTH)),
                  _const_spec((1, ATTN_WIDTH)),
                  _const_spec((ATTN_WIDTH, ATTN_WIDTH))],
        out_specs=[tile(w) for w in outs],
        out_shape=[jax.ShapeDtypeStruct((bsz, seq, w), F32) for w in outs],
        compiler_params=_cparams("parallel", "parallel"),
    )(x, x, x, mod, norm1_g.reshape(1, -1), w_in_bf16, conv_w, conv_b.reshape(1, -1),
      q_norm_g.reshape(1, -1), k_norm_g.reshape(1, -1), jnp.asarray(blockdiag))


def _attn_kernel(*refs, tq, sub, dil):
    npair = PAIRS_PER_GROUP
    q_refs = refs[0:npair]
    k_refs = refs[npair:4 * npair]
    v_refs = refs[4 * npair:7 * npair]
    tb_ref = refs[7 * npair]
    o_refs = refs[7 * npair + 1:8 * npair + 1]
    l_refs = refs[8 * npair + 1:9 * npair + 1]
    t0 = pl.program_id(1) * tq
    lower = lax.broadcasted_iota(jnp.int32, (1, PAIR_WIDTH), 1) < HEAD_DIM

    def rows(r, count, offset=0):
        if dil == 1:
            return pl.ds(offset, count)
        return pl.ds(r + offset * dil, count, stride=dil)

    def residue(r):
        for pair in range(npair):
            kp, kc, kn = k_refs[3 * pair:3 * pair + 3]
            vp, vc, vn = v_refs[3 * pair:3 * pair + 3]
            kfull = jnp.concatenate([kp[0, rows(r, RADIUS), :], kc[0, rows(r, tq), :], kn[0, rows(r, RADIUS), :]],
                                    axis=0).astype(BF16)
            vfull = jnp.concatenate([vp[0, rows(r, RADIUS), :], vc[0, rows(r, tq), :], vn[0, rows(r, RADIUS), :]],
                                    axis=0).astype(BF16)
            for j in range(tq // QBLK):
                kpos = t0 - RADIUS + QBLK * j + lax.broadcasted_iota(jnp.int32, (1, 2 * QBLK), 1)
                valid = (kpos >= 0) & (kpos < sub)
                win = slice(QBLK * j, QBLK * (j + 2))
                q2 = q_refs[pair][0, rows(r, QBLK, QBLK * j), :]
                k2 = kfull[win]
                v2 = vfull[win]
                res = []
                for hh in range(2):
                    sel = lower if hh == 0 else jnp.logical_not(lower)
                    qm = jnp.where(sel, q2, 0.0).astype(BF16)
                    s = lax.dot_general(qm, k2, (((1,), (1,)), ((), ())), preferred_element_type=F32)
                    s = jnp.where(valid, s + tb_ref[0, 2 * pair + hh], NEG_INF)
                    m = jnp.max(s, axis=-1, keepdims=True)
                    p = jnp.exp(s - m)
                    den = jnp.sum(p, axis=-1, keepdims=True)
                    res.append((_dot(p.astype(BF16), v2) / den, m + jnp.log(den)))
                o_refs[pair][0, rows(r, QBLK, QBLK * j), :] = jnp.where(lower, res[0][0], res[1][0])
                l_refs[pair][0, rows(r, QBLK, QBLK * j), :] = jnp.where(lower, res[0][1], res[1][1])

    if dil == 1:
        residue(0)
    else:
        def body(r, carry):
            residue(r)
            return carry
        lax.fori_loop(0, dil, body, 0, unroll=2)


def _dilated_group(q, k, v, tb, gi, dil, tq):
    bsz, seq, _ = q.shape
    sub = seq // dil
    tq = min(tq, sub)
    tp = tq * dil
    halo = RADIUS * dil
    hb = tp // halo
    lane0 = gi * PAIRS_PER_GROUP
    cur = lambda p: pl.BlockSpec((1, tp, PAIR_WIDTH), lambda b, i: (b, i, lane0 + p))
    prev = lambda p: pl.BlockSpec((1, halo, PAIR_WIDTH), lambda b, i: (b, jnp.maximum(i * hb - 1, 0), lane0 + p))
    nxt = lambda p: pl.BlockSpec((1, halo, PAIR_WIDTH),
                                 lambda b, i: (b, jnp.minimum((i + 1) * hb, seq // halo - 1), lane0 + p))
    out = pl.BlockSpec((1, tp, PAIR_WIDTH), lambda b, i: (b, i, 0))
    pairs = range(PAIRS_PER_GROUP)
    halo_specs = [s(p) for p in pairs for s in (prev, cur, nxt)]
    res = pl.pallas_call(
        functools.partial(_attn_kernel, tq=tq, sub=sub, dil=dil),
        name="attn",
        grid=(bsz, seq // tp),
        in_specs=[cur(p) for p in pairs] + halo_specs + halo_specs
                 + [pl.BlockSpec((1, HEADS_PER_GROUP, QBLK, 2 * QBLK), lambda b, i: (gi, 0, 0, 0))],
        out_specs=[out] * (2 * PAIRS_PER_GROUP),
        out_shape=[jax.ShapeDtypeStruct((bsz, seq, PAIR_WIDTH), F32)] * (2 * PAIRS_PER_GROUP),
        compiler_params=_cparams("parallel", "parallel"),
    )(*([q] * PAIRS_PER_GROUP + [k] * (3 * PAIRS_PER_GROUP) + [v] * (3 * PAIRS_PER_GROUP) + [tb]))
    return res[:PAIRS_PER_GROUP], res[PAIRS_PER_GROUP:]


def _merge_kernel(*refs):
    ng = len(DILATED_GROUPS)
    npair = PAIRS_PER_GROUP
    x_ref, conv_ref, w_ref, x0_ref = refs[0:4]
    o_refs = refs[4:4 + ng * npair]
    l_refs = refs[4 + ng * npair:4 + 2 * ng * npair]
    g_ref, mod_ref, d_ref, g2_ref, whb_ref, wab_ref, wo_ref, wup_ref, wdn_ref, out_ref = refs[4 + 2 * ng * npair:]
    mod = mod_ref[0]
    gt1 = mod[:, 2 * D_MODEL:3 * D_MODEL]
    sh2 = mod[:, 3 * D_MODEL:4 * D_MODEL]
    sc2 = mod[:, 4 * D_MODEL:5 * D_MODEL]
    gt2 = mod[:, 5 * D_MODEL:6 * D_MODEL]

    merged = []
    for p in range(npair):
        lses = [l_refs[g * npair + p][0] for g in range(ng)]
        lm = functools.reduce(jnp.maximum, lses)
        es = [jnp.exp(l - lm) for l in lses]
        num = sum(e * o_refs[g * npair + p][0] for g, e in enumerate(es))
        merged.append(num / sum(es))
    y_at = jnp.concatenate(merged, axis=-1)

    y_hy = x0_ref[0] * (conv_ref[0] + w_ref[0] * d_ref[...])
    g = g_ref[0]
    mix = g[:, :D_MODEL] * _dot(y_hy.astype(BF16), whb_ref[...]) \
        + g[:, D_MODEL:] * _dot(y_at.astype(BF16), wab_ref[...])
    h = x_ref[0] + gt1 * _dot(mix.astype(BF16), wo_ref[...])

    ms = jnp.mean(h * h, axis=-1, keepdims=True)
    u2 = (h * lax.rsqrt(ms + EPS)) * (g2_ref[...] * (1.0 + sc2)) + sh2
    up = jnp.maximum(_dot(u2.astype(BF16), wup_ref[...]), 0.0)
    ff = _dot((up * up).astype(BF16), wdn_ref[...])
    out_ref[0] = h + gt2 * ff


def _merge_mlp(x, conv, w, x0, outs, lses, gates, mod, hyena_d, norm2_g, w_hy_br, w_at_br, w_out, w_up, w_down,
               tm):
    bsz, seq, _ = x.shape
    tile = lambda w: pl.BlockSpec((1, tm, w), lambda b, i: (b, i, 0))
    n_attn = len(outs) + len(lses)
    return pl.pallas_call(
        _merge_kernel,
        name="merge_mlp",
        grid=(bsz, seq // tm),
        in_specs=[tile(D_MODEL)] + [tile(D_HYENA)] * 3 + [tile(PAIR_WIDTH)] * n_attn + [tile(2 * D_MODEL),
                  pl.BlockSpec((1, 1, 6 * D_MODEL), lambda b, i: (b, 0, 0)),
                  _const_spec((1, D_HYENA)),
                  _const_spec((1, D_MODEL)),
                  _const_spec((D_HYENA, D_MODEL)),
                  _const_spec((PAIRS_PER_GROUP * PAIR_WIDTH, D_MODEL)),
                  _const_spec((D_MODEL, D_MODEL)),
                  _const_spec((D_MODEL, D_FF)),
                  _const_spec((D_FF, D_MODEL))],
        out_specs=tile(D_MODEL),
        out_shape=jax.ShapeDtypeStruct(x.shape, F32),
        compiler_params=_cparams("parallel", "parallel"),
    )(x, conv, w, x0, *outs, *lses, gates, mod, hyena_d.reshape(1, -1), norm2_g.reshape(1, -1), w_hy_br, w_at_br,
      w_out, w_up, w_down)


ATTN_TQ = {1: 512, 4: 256, 16: 128}


def _long_conv(w, kspec, consts):
    a = _dft_stage1(w, consts, is_filter=False)
    qf = _dft_stage2(a, kspec, consts["g2"], consts["g2i"], 5)
    return _dft_stage3(qf, consts)


def _filter_spec(seq, consts, filt):
    taps, asum = _filter_taps(seq, *filt)
    af = _dft_stage1(taps[None], consts, is_filter=True)
    return _filter_spectrum(af, consts["g2"], asum, 5)


def _encoder_layer(x, mod, tb, p):
    bsz, seq, _ = x.shape
    consts = {k: jnp.asarray(v) for k, v in _dft_consts(seq).items()}
    mod3 = mod.reshape(bsz, 1, 6 * D_MODEL)
    w, x0, q, k, v, gates = _in_projection(x, mod3, p["norm1_g"], p["w_in"], p["conv_w"], p["conv_b"],
                                           p["q_norm_g"], p["k_norm_g"], tm=256)
    kspec = _filter_spec(seq, consts, p["filt"])
    conv = _long_conv(w, kspec, consts)
    outs, lses = [], []
    for gi, (_, dil) in enumerate(DILATED_GROUPS):
        o, lse = _dilated_group(q, k, v, tb, gi, dil, ATTN_TQ[dil])
        outs.extend(o)
        lses.extend(lse)
    return _merge_mlp(x, conv, w, x0, outs, lses, gates, mod3, p["hyena_d"], p["norm2_g"], p["w_hy_br"],
                      p["w_at_br"], p["w_out"], p["w_up"], p["w_down"], tm=256)


def kernel(x_prompt, x_sample, c_prompt, c_sample, rel_bias, ada_w, ada_b, norm1_g, w_in, conv_w, conv_b,
           filt_w1, filt_b1, filt_w2, filt_b2, filt_w3, filt_b3, filt_freq, filt_w_out, hyena_d, q_norm_g,
           k_norm_g, w_hy_br, w_at_br, w_out, norm2_g, w_up, w_down):
    depth = ada_w.shape[0]
    tb = _bias_tables(rel_bias)
    y_prompt, y_sample = x_prompt, x_sample
    nbp = c_prompt.shape[0]
    for l in range(depth):
        p = dict(norm1_g=norm1_g[l], w_in=w_in[l].astype(BF16), conv_w=conv_w[l], conv_b=conv_b[l],
                 filt=(filt_w1[l], filt_b1[l], filt_w2[l], filt_b2[l], filt_w3[l], filt_b3[l], filt_freq[l],
                       filt_w_out[l]),
                 hyena_d=hyena_d[l], q_norm_g=q_norm_g[l], k_norm_g=k_norm_g[l],
                 w_hy_br=w_hy_br[l].astype(BF16), w_at_br=w_at_br[l].astype(BF16), w_out=w_out[l].astype(BF16),
                 norm2_g=norm2_g[l], w_up=w_up[l].astype(BF16), w_down=w_down[l].astype(BF16))
        mod = _modulation(jnp.concatenate([c_prompt, c_sample], axis=0), ada_w[l], ada_b[l])
        y_prompt = _encoder_layer(y_prompt, mod[:nbp], tb, p)
        y_sample = _encoder_layer(y_sample, mod[nbp:], tb, p)
    return (y_prompt, y_sample)
```

```python
import functools
import math

import numpy as np
import jax
import jax.numpy as jnp
from jax import lax
from jax.experimental import pallas as pl
from jax.experimental.pallas import tpu as pltpu

F32 = jnp.float32
BF16 = jnp.bfloat16
HIGHEST = lax.Precision.HIGHEST

D_MODEL = 1024
EPS = 1e-6
HEAD_DIM = 64
N_HEADS = 12
DILATED_GROUPS = ((128, 1), (512, 4), (2048, 16))
HEADS_PER_GROUP = 4
ATTN_WIDTH = 768
PAIR_WIDTH = 2 * HEAD_DIM
PAIRS_PER_GROUP = HEADS_PER_GROUP // 2
RADIUS = 64
NUM_BUCKETS = 32
MAX_DISTANCE = 1024
NEG_INF = -1e30
D_HYENA = 768
FILTER_BANDS = 16
FILTER_HIDDEN = 64
DECAY_TARGET = 1e-2
FAST_DECAY_PCT = 0.3
SLOW_DECAY_PCT = 1.5
D_FF = 4096
IN_PROJ_WIDTH = 6656

SUBLANES = 8
DFT_N1 = 128
DFT_H1 = 64
DFT_KP = 65
DFT_ROWS = DFT_KP * SUBLANES
QBLK = 128
VMEM_LIMIT = 56 * 1024 * 1024


def _cparams(*sem):
    return pltpu.CompilerParams(dimension_semantics=sem, vmem_limit_bytes=VMEM_LIMIT)


def _const_spec(shape):
    nd = len(shape)
    return pl.BlockSpec(shape, lambda *_: (0,) * nd, pipeline_mode=pl.Buffered(1))


def _dot(a, b, precision=None):
    return jnp.dot(a, b, preferred_element_type=F32, precision=precision)


@functools.lru_cache(maxsize=None)
def _dft_consts(L):
    n2n = L // DFT_H1
    n = 2 * L
    nch = n2n // SUBLANES
    r = np.arange(SUBLANES)
    eye = np.eye(SUBLANES)
    k1 = np.arange(DFT_KP)
    n1 = np.arange(DFT_H1)
    ang = -2.0 * np.pi * (r[:, None, None] * k1[None, :, None] / n + k1[None, :, None] * n1[None, None, :] / DFT_N1)
    f1 = np.stack([np.cos(ang), np.sin(ang)], axis=1)
    f0 = np.einsum("rpkn,rs->pkrns", f1, eye).reshape(2 * DFT_ROWS, DFT_H1 * SUBLANES)
    angc = 2.0 * np.pi * (n1[None, :, None] * k1[None, None, :] / DFT_N1 + r[:, None, None] * k1[None, None, :] / n)
    mult = np.full(DFT_KP, 2.0)
    mult[0] = mult[-1] = 1.0
    mi = np.stack([mult * np.cos(angc), -mult * np.sin(angc)], axis=2) / n
    m0 = np.einsum("rnpk,rs->nrpks", mi, eye).reshape(DFT_H1 * SUBLANES, 2 * DFT_ROWS)
    theta = -2.0 * np.pi * SUBLANES * np.arange(nch)[:, None] * k1[None, :] / n
    rep = lambda a: np.repeat(a, SUBLANES, axis=1)[:, :, None].astype(np.float32)
    sign = np.repeat(np.where(k1 % 2 == 0, 1.0, -1.0), SUBLANES)[:, None].astype(np.float32)
    a2 = -2.0 * np.pi * np.arange(n2n)[:, None] * np.arange(n2n)[None, :] / n2n
    fr, fi = np.cos(a2), np.sin(a2)
    g2 = np.block([[fr, -fi], [fi, fr]])
    g2i = np.block([[fr, fi], [-fi, fr]])
    as_bf16 = lambda a: np.asarray(a, np.float32).astype(BF16)
    f0p = np.einsum("rpkn,rs->kprns", f1, eye).reshape(2 * DFT_ROWS, DFT_H1 * SUBLANES)
    m0p = np.einsum("rnpk,rs->nrkps", mi, eye).reshape(DFT_H1 * SUBLANES, 2 * DFT_ROWS)
    jj, pp, rr = np.meshgrid(np.arange(nch), np.arange(2), r, indexing="ij")
    perm = (pp * n2n + SUBLANES * jj + rr).reshape(-1)
    per_k1 = lambda a: a[:, :, None, None].astype(np.float32)
    return dict(f0=as_bf16(f0), m0p=as_bf16(m0p), f0p=as_bf16(f0p), rot_c=rep(np.cos(theta)),
                rot_s=rep(np.sin(theta)), rotp_c=per_k1(np.cos(theta)), rotp_s=per_k1(np.sin(theta)), sign=sign,
                g2=as_bf16(g2), g2p=as_bf16(g2[:, perm]), g2ip=as_bf16(g2i[perm, :]))


def _t5_bucket_np(rel):
    half = NUM_BUCKETS // 2
    max_exact = half // 2
    n = np.abs(rel)
    ret = np.where(rel > 0, half, 0)
    large = max_exact + (np.log(np.maximum(n, 1).astype(np.float32) / np.float32(max_exact))
                         / np.float32(math.log(MAX_DISTANCE / max_exact))
                         * np.float32(half - max_exact)).astype(np.int32)
    large = np.minimum(large, half - 1)
    return ret + np.where(n < max_exact, n, large)


@functools.lru_cache(maxsize=None)
def _bucket_tables():
    qi = np.arange(QBLK)[:, None]
    c = np.arange(2 * QBLK)[None, :]
    rel = c - RADIUS - qi
    tabs = []
    for _, dil in DILATED_GROUPS:
        b = _t5_bucket_np(rel * dil)
        tabs.append(np.where(np.abs(rel) <= RADIUS, b, -1))
    return np.stack(tabs).astype(np.int32)


@functools.lru_cache(maxsize=None)
def _small_consts():
    bands = np.linspace(1e-4, FILTER_BANDS - 1, FILTER_BANDS, dtype=np.float32)[None, :]
    deltas = np.abs(np.linspace(math.log(DECAY_TARGET) / SLOW_DECAY_PCT,
                                math.log(DECAY_TARGET) / FAST_DECAY_PCT, D_HYENA, dtype=np.float32))[None, :]
    head = np.arange(ATTN_WIDTH) // HEAD_DIM
    blockdiag = (head[:, None] == head[None, :]).astype(np.float32) / HEAD_DIM
    return bands, deltas, blockdiag.astype(BF16)


def _mod_kernel(c_ref, w_ref, b_ref, o_ref):
    c = c_ref[...]
    s = c / (1.0 + jnp.exp(-c))
    o_ref[...] = _dot(s, w_ref[...], HIGHEST) + b_ref[...]


def _modulation(c, ada_w, ada_b):
    nb, _ = c.shape
    nw = ada_w.shape[1]
    tn = 1024
    return pl.pallas_call(
        _mod_kernel,
        name="mod",
        grid=(nw // tn,),
        in_specs=[pl.BlockSpec((nb, D_MODEL), lambda j: (0, 0)),
                  pl.BlockSpec((D_MODEL, tn), lambda j: (0, j)),
                  pl.BlockSpec((1, tn), lambda j: (0, j))],
        out_specs=pl.BlockSpec((nb, tn), lambda j: (0, j)),
        out_shape=jax.ShapeDtypeStruct((nb, nw), F32),
        compiler_params=_cparams("parallel"),
    )(c, ada_w, ada_b.reshape(1, nw))


def _bias_kernel(rb_ref, bk_ref, o_ref):
    g = pl.program_id(0)
    bk = bk_ref[0]
    for h in range(HEADS_PER_GROUP):
        acc = jnp.full(bk.shape, NEG_INF, F32)
        for b in range(NUM_BUCKETS):
            acc = jnp.where(bk == b, rb_ref[b, g * HEADS_PER_GROUP + h], acc)
        o_ref[0, h] = acc


def _bias_tables(rel_bias):
    bk = jnp.asarray(_bucket_tables())
    ng = len(DILATED_GROUPS)
    return pl.pallas_call(
        _bias_kernel,
        name="bias_tab",
        grid=(ng,),
        in_specs=[pl.BlockSpec(memory_space=pltpu.SMEM),
                  pl.BlockSpec((1, QBLK, 2 * QBLK), lambda g: (g, 0, 0))],
        out_specs=pl.BlockSpec((1, HEADS_PER_GROUP, QBLK, 2 * QBLK), lambda g: (g, 0, 0, 0)),
        out_shape=jax.ShapeDtypeStruct((ng, HEADS_PER_GROUP, QBLK, 2 * QBLK), F32),
        compiler_params=_cparams("arbitrary"),
    )(rel_bias, bk)


def _filter_kernel(w1t_ref, w1c_ref, w1s_ref, b1_ref, w2_ref, b2_ref, w3_ref, b3_ref, fr_ref, wo_ref,
                   bands_ref, deltas_ref, k_ref, asum_ref, *, seq, rows):
    i = pl.program_id(0)

    def lag(shape, axis):
        m = i * rows + lax.broadcasted_iota(jnp.int32, shape, axis)
        return m, jnp.where(m < seq, m, 2 * seq - m).astype(F32)

    _, pos_l = lag((1, rows), 1)
    t_l = pos_l / float(seq - 1)
    arg = bands_ref[...] * ((2.0 * math.pi) * pos_l / float(seq))
    fr = fr_ref[...]
    z = w1t_ref[...] * t_l + _dot(w1c_ref[...], jnp.cos(arg), HIGHEST) \
        + _dot(w1s_ref[...], -jnp.sin(arg), HIGHEST) + b1_ref[...]
    h = jnp.sin(fr * z)
    h = jnp.sin(fr * (_dot(w2_ref[...], h, HIGHEST) + b2_ref[...]))
    h = jnp.sin(fr * (_dot(w3_ref[...], h, HIGHEST) + b3_ref[...]))
    m, pos = lag((rows, 1), 0)
    t = pos / float(seq - 1)
    k = _dot(h.T, wo_ref[...], HIGHEST) * jnp.exp(-t * deltas_ref[...])
    k = jnp.where(m == seq, 0.0, k)
    k_ref[...] = k

    @pl.when(i == 0)
    def _():
        asum_ref[...] = jnp.zeros_like(asum_ref)

    asum_ref[...] += jnp.sum(jnp.abs(k), axis=0, keepdims=True)


def _filter_taps(seq, w1, b1, w2, b2, w3, b3, freq, w_out):
    bands, deltas, _ = _small_consts()
    rows = 1024
    nblk = 2 * seq // rows
    half = nblk // 2
    hid = FILTER_HIDDEN
    col = lambda a: a.reshape(-1, 1)
    small = lambda shape: pl.BlockSpec(shape, lambda i: (0, 0))
    return pl.pallas_call(
        functools.partial(_filter_kernel, seq=seq, rows=rows),
        name="filter_taps",
        grid=(nblk,),
        in_specs=[small((hid, 1)), small((hid, FILTER_BANDS)), small((hid, FILTER_BANDS)), small((hid, 1)),
                  small((hid, hid)), small((hid, 1)), small((hid, hid)), small((hid, 1)), small((hid, 1)),
                  pl.BlockSpec((hid, D_HYENA), lambda i: (0, jnp.where(i >= half, 1, 0))),
                  small((FILTER_BANDS, 1)), small((1, D_HYENA))],
        out_specs=[pl.BlockSpec((rows, D_HYENA), lambda i: (i, 0)),
                   pl.BlockSpec((1, D_HYENA), lambda i: (0, 0))],
        out_shape=[jax.ShapeDtypeStruct((2 * seq, D_HYENA), F32),
                   jax.ShapeDtypeStruct((1, D_HYENA), F32)],
        compiler_params=_cparams("arbitrary"),
    )(w1[0:1].T, w1[1:1 + FILTER_BANDS].T, w1[1 + FILTER_BANDS:].T, col(b1), w2.T, col(b2), w3.T, col(b3),
      col(freq), w_out, jnp.asarray(bands.T), jnp.asarray(deltas))


def _rotate(re, im, c, s):
    return c * re - s * im, s * re + c * im


def _store_spectrum_block(o_ref, re, im):
    c = re.shape[-1]
    o_ref[0, 0] = re.reshape(DFT_KP, SUBLANES, c)
    o_ref[0, 1] = im.reshape(DFT_KP, SUBLANES, c)


def _dft1_kernel(x_ref, f0_ref, rc_ref, rs_ref, o_ref):
    c = x_ref.shape[-1]
    x = x_ref[0].reshape(DFT_H1 * SUBLANES, c).astype(BF16)
    a = _dot(f0_ref[...], x).reshape(DFT_KP, 2, SUBLANES, c)
    re, im = _rotate(a[:, 0], a[:, 1], rc_ref[0], rs_ref[0])
    o_ref[0, :, 0] = jnp.concatenate([re, im], axis=1).astype(BF16)


def _dft1_filter_kernel(lo_ref, hi_ref, f0_ref, rc_ref, rs_ref, sign_ref, o_ref):
    c = lo_ref.shape[-1]
    flat = lambda ref: ref[0].reshape(DFT_H1 * SUBLANES, c).astype(BF16)
    lo = _dot(f0_ref[...], flat(lo_ref))
    hi = _dot(f0_ref[...], flat(hi_ref))
    sign = sign_ref[...]
    re, im = _rotate(lo[:DFT_ROWS] + sign * hi[:DFT_ROWS], lo[DFT_ROWS:] + sign * hi[DFT_ROWS:],
                     rc_ref[0], rs_ref[0])
    _store_spectrum_block(o_ref, re, im)


def _dft_stage1(x, consts, is_filter):
    bsz, rows, c = x.shape
    n2n = rows // (DFT_N1 if is_filter else DFT_H1)
    nch = n2n // SUBLANES
    rot = pl.BlockSpec((1, DFT_ROWS, 1), lambda j, b: (j, 0, 0))
    f0 = _const_spec((2 * DFT_ROWS, DFT_H1 * SUBLANES))
    out_spec = pl.BlockSpec((1, 2, DFT_KP, SUBLANES, c), lambda j, b: (b, 0, 0, j, 0))
    out_shape = jax.ShapeDtypeStruct((bsz, 2, DFT_KP, n2n, c), F32)
    if is_filter:
        xv = x.reshape(2, DFT_H1, n2n, c)
        data = [pl.BlockSpec((1, DFT_H1, SUBLANES, c), lambda j, b: (0, 0, j, 0)),
                pl.BlockSpec((1, DFT_H1, SUBLANES, c), lambda j, b: (1, 0, j, 0))]
        args = (xv, xv, consts["f0"], consts["rot_c"], consts["rot_s"], consts["sign"])
        specs = data + [f0, rot, rot, _const_spec((DFT_ROWS, 1))]
        body = _dft1_filter_kernel
    else:
        xv = x.reshape(bsz, DFT_H1, n2n, c)
        rotp = pl.BlockSpec((1, DFT_KP, 1, 1), lambda j, b: (j, 0, 0, 0))
        args = (xv, consts["f0p"], consts["rotp_c"], consts["rotp_s"])
        specs = [pl.BlockSpec((1, DFT_H1, SUBLANES, c), lambda j, b: (b, 0, j, 0)), f0, rotp, rotp]
        body = _dft1_kernel
        out_spec = pl.BlockSpec((1, DFT_KP, 1, 2 * SUBLANES, c), lambda j, b: (b, 0, j, 0, 0))
        out_shape = jax.ShapeDtypeStruct((bsz, DFT_KP, nch, 2 * SUBLANES, c), BF16)
    return pl.pallas_call(
        body,
        name="dft1",
        grid=(nch, bsz),
        in_specs=specs,
        out_specs=out_spec,
        out_shape=out_shape,
        compiler_params=_cparams("parallel", "parallel"),
    )(*args)


def _filter_spec_kernel(a_ref, g2_ref, asum_ref, o_ref, *, kb, n2n):
    scale = 1.0 / asum_ref[...]
    for j in range(kb):
        a = jnp.concatenate([a_ref[0, 0, j], a_ref[0, 1, j]], axis=0).astype(BF16)
        b = _dot(g2_ref[...], a) * scale
        o_ref[0, j] = b[:n2n]
        o_ref[1, j] = b[n2n:]


def _filter_spectrum(af, g2, asum, kb):
    _, _, kp, n2n, c = af.shape
    return pl.pallas_call(
        functools.partial(_filter_spec_kernel, kb=kb, n2n=n2n),
        name="filter_spec",
        grid=(kp // kb,),
        in_specs=[pl.BlockSpec((1, 2, kb, n2n, c), lambda i: (0, 0, i, 0, 0)),
                  pl.BlockSpec((2 * n2n, 2 * n2n), lambda i: (0, 0)),
                  pl.BlockSpec((1, c), lambda i: (0, 0))],
        out_specs=pl.BlockSpec((2, kb, n2n, c), lambda i: (0, i, 0, 0)),
        out_shape=jax.ShapeDtypeStruct((2, kp, n2n, c), F32),
        compiler_params=_cparams("parallel"),
    )(af, g2, asum)


def _dft2_kernel(a_ref, ks_ref, g2_ref, g2i_ref, o_ref, *, kb, n2n):
    nch, rows, c = a_ref.shape[2:]
    for j in range(kb):
        a = a_ref[0, j].reshape(2 * n2n, c)
        b = _dot(g2_ref[...], a)
        br, bi = b[:n2n], b[n2n:]
        kr, ki = ks_ref[0, j], ks_ref[1, j]
        p = jnp.concatenate([br * kr - bi * ki, br * ki + bi * kr], axis=0).astype(BF16)
        q = _dot(g2i_ref[...], p)
        o_ref[0, j] = q.astype(BF16).reshape(nch, rows, c)


def _dft_stage2(a, kspec, g2p, g2ip, kb):
    bsz, kp, nch, rows, c = a.shape
    n2n = nch * SUBLANES
    blk = pl.BlockSpec((1, kb, nch, rows, c), lambda i, b: (b, i, 0, 0, 0))
    return pl.pallas_call(
        functools.partial(_dft2_kernel, kb=kb, n2n=n2n),
        name="dft2",
        grid=(kp // kb, bsz),
        in_specs=[blk,
                  pl.BlockSpec((2, kb, n2n, c), lambda i, b: (0, i, 0, 0)),
                  pl.BlockSpec((2 * n2n, 2 * n2n), lambda i, b: (0, 0)),
                  pl.BlockSpec((2 * n2n, 2 * n2n), lambda i, b: (0, 0))],
        out_specs=blk,
        out_shape=jax.ShapeDtypeStruct(a.shape, BF16),
        compiler_params=_cparams("parallel", "parallel"),
    )(a, kspec, g2p, g2ip)


def _dft3_kernel(q_ref, m0_ref, rc_ref, rs_ref, y_ref):
    c = y_ref.shape[-1]
    q = q_ref[0, :, 0].astype(F32).reshape(DFT_KP, 2, SUBLANES, c)
    re, im = _rotate(q[:, 0], q[:, 1], rc_ref[0], -rs_ref[0])
    q = jnp.concatenate([re, im], axis=1).reshape(2 * DFT_ROWS, c).astype(BF16)
    y_ref[0] = _dot(m0_ref[...], q).reshape(DFT_H1, SUBLANES, c)


def _dft_stage3(q, consts):
    bsz, _, nch, rows, c = q.shape
    n2n = nch * SUBLANES
    rot = pl.BlockSpec((1, DFT_KP, 1, 1), lambda j, b: (j, 0, 0, 0))
    y = pl.pallas_call(
        _dft3_kernel,
        name="dft3",
        grid=(nch, bsz),
        in_specs=[pl.BlockSpec((1, DFT_KP, 1, rows, c), lambda j, b: (b, 0, j, 0, 0)),
                  _const_spec((DFT_H1 * SUBLANES, 2 * DFT_ROWS)), rot, rot],
        out_specs=pl.BlockSpec((1, DFT_H1, SUBLANES, c), lambda j, b: (b, 0, j, 0)),
        out_shape=jax.ShapeDtypeStruct((bsz, DFT_H1, n2n, c), F32),
        compiler_params=_cparams("parallel", "parallel"),
    )(q, consts["m0p"], consts["rotp_c"], consts["rotp_s"])
    return y.reshape(bsz, DFT_H1 * n2n, c)


def _inproj_kernel(x_ref, xp_ref, xn_ref, mod_ref, g1_ref, win_ref, cw_ref, cb_ref, gq_ref, gk_ref, bd_ref,
                   w_out, x0_out, q_out, k_out, v_out, g_out, *, tm, nt):
    i = pl.program_id(1)
    mod = mod_ref[0]
    sh1 = mod[:, 0:D_MODEL]
    scale1 = g1_ref[...] * (1.0 + mod[:, D_MODEL:2 * D_MODEL])

    def norm_mod(x):
        ms = jnp.mean(x * x, axis=-1, keepdims=True)
        return ((x * lax.rsqrt(ms + EPS)) * scale1 + sh1).astype(BF16)

    u = norm_mod(x_ref[0])
    uh = norm_mod(jnp.concatenate([xp_ref[0], xn_ref[0]], axis=0))
    ue = jnp.concatenate([u, uh], axis=0)
    row = lax.broadcasted_iota(jnp.int32, (tm, 1), 0)

    def conv_chunk(c):
        cols = slice(c * D_HYENA, (c + 1) * D_HYENA)
        ze = _dot(ue, win_ref[:, cols])
        z, zh = ze[:tm], ze[tm:]
        prev = jnp.where(i > 0, zh[7:8], 0.0)
        nxt = jnp.where(i < nt - 1, zh[8:9], 0.0)
        zm = jnp.where(row == 0, prev, pltpu.roll(z, 1, 0))
        zp = jnp.where(row == tm - 1, nxt, pltpu.roll(z, tm - 1, 0))
        return zm * cw_ref[0:1, cols] + z * cw_ref[1:2, cols] + zp * cw_ref[2:3, cols] + cb_ref[:, cols]

    x0_out[0] = conv_chunk(0)
    w_out[0] = conv_chunk(1) * conv_chunk(2)

    base = 3 * D_HYENA

    def head_norm(z, gain):
        ms = _dot((z * z).astype(BF16), bd_ref[...])
        return (z * lax.rsqrt(ms + EPS)) * gain

    zq = _dot(u, win_ref[:, base:base + ATTN_WIDTH])
    q_out[0] = head_norm(zq, gq_ref[...]) * (HEAD_DIM ** -0.5)
    zk = _dot(u, win_ref[:, base + ATTN_WIDTH:base + 2 * ATTN_WIDTH])
    k_out[0] = head_norm(zk, gk_ref[...])
    v_out[0] = _dot(u, win_ref[:, base + 2 * ATTN_WIDTH:base + 3 * ATTN_WIDTH])
    zg = _dot(u, win_ref[:, base + 3 * ATTN_WIDTH:])
    g_out[0] = 1.0 / (1.0 + jnp.exp(-zg))


def _in_projection(x, mod, norm1_g, w_in_bf16, conv_w, conv_b, q_norm_g, k_norm_g, tm):
    bsz, seq, _ = x.shape
    nt = seq // tm
    hb = tm // SUBLANES
    _, _, blockdiag = _small_consts()
    tile = lambda w: pl.BlockSpec((1, tm, w), lambda b, i: (b, i, 0))
    outs = [D_HYENA, D_HYENA, ATTN_WIDTH, ATTN_WIDTH, ATTN_WIDTH, 2 * D_MODEL]
    return pl.pallas_call(
        functools.partial(_inproj_kernel, tm=tm, nt=nt),
        name="inproj",
        grid=(bsz, nt),
        in_specs=[tile(D_MODEL),
                  pl.BlockSpec((1, SUBLANES, D_MODEL), lambda b, i: (b, jnp.maximum(i * hb - 1, 0), 0)),
                  pl.BlockSpec((1, SUBLANES, D_MODEL),
                               lambda b, i: (b, jnp.minimum((i + 1) * hb, seq // SUBLANES - 1), 0)),
                  pl.BlockSpec((1, 1, 6 * D_MODEL), lambda b, i: (b, 0, 0)),
                  _const_spec((1, D_MODEL)),
                  _const_spec((D_MODEL, IN_PROJ_WIDTH)),
                  _const_spec((3, 3 * D_HYENA)),
                  _const_spec((1, 3 * D_HYENA)),
                  _const_spec((1, ATTN_WIDTH)),
                  _const_spec((1, ATTN_WIDTH)),
                  _const_spec((ATTN_WIDTH, ATTN_WIDTH))],
        out_specs=[tile(w) for w in outs],
        out_shape=[jax.ShapeDtypeStruct((bsz, seq, w), F32) for w in outs],
        compiler_params=_cparams("parallel", "parallel"),
    )(x, x, x, mod, norm1_g.reshape(1, -1), w_in_bf16, conv_w, conv_b.reshape(1, -1),
      q_norm_g.reshape(1, -1), k_norm_g.reshape(1, -1), jnp.asarray(blockdiag))


def _attn_kernel(*refs, tq, sub, dil):
    npair = PAIRS_PER_GROUP
    q_refs = refs[0:npair]
    k_refs = refs[npair:4 * npair]
    v_refs = refs[4 * npair:7 * npair]
    tb_ref = refs[7 * npair]
    o_refs = refs[7 * npair + 1:8 * npair + 1]
    l_refs = refs[8 * npair + 1:9 * npair + 1]
    t0 = pl.program_id(1) * tq
    lower = lax.broadcasted_iota(jnp.int32, (1, PAIR_WIDTH), 1) < HEAD_DIM

    def rows(r, count, offset=0):
        if dil == 1:
            return pl.ds(offset, count)
        return pl.ds(r + offset * dil, count, stride=dil)

    def residue(r):
        for pair in range(npair):
            kp, kc, kn = k_refs[3 * pair:3 * pair + 3]
            vp, vc, vn = v_refs[3 * pair:3 * pair + 3]
            kfull = jnp.concatenate([kp[0, rows(r, RADIUS), :], kc[0, rows(r, tq), :], kn[0, rows(r, RADIUS), :]],
                                    axis=0).astype(BF16)
            vfull = jnp.concatenate([vp[0, rows(r, RADIUS), :], vc[0, rows(r, tq), :], vn[0, rows(r, RADIUS), :]],
                                    axis=0).astype(BF16)
            for j in range(tq // QBLK):
                kpos = t0 - RADIUS + QBLK * j + lax.broadcasted_iota(jnp.int32, (1, 2 * QBLK), 1)
                valid = (kpos >= 0) & (kpos < sub)
                win = slice(QBLK * j, QBLK * (j + 2))
                q2 = q_refs[pair][0, rows(r, QBLK, QBLK * j), :]
                k2 = kfull[win]
                v2 = vfull[win]
                res = []
                for hh in range(2):
                    sel = lower if hh == 0 else jnp.logical_not(lower)
                    qm = jnp.where(sel, q2, 0.0).astype(BF16)
                    s = lax.dot_general(qm, k2, (((1,), (1,)), ((), ())), preferred_element_type=F32)
                    s = jnp.where(valid, s + tb_ref[0, 2 * pair + hh], NEG_INF)
                    m = jnp.max(s, axis=-1, keepdims=True)
                    p = jnp.exp(s - m)
                    den = jnp.sum(p, axis=-1, keepdims=True)
                    res.append((_dot(p.astype(BF16), v2) / den, m + jnp.log(den)))
                o_refs[pair][0, rows(r, QBLK, QBLK * j), :] = jnp.where(lower, res[0][0], res[1][0])
                l_refs[pair][0, rows(r, QBLK, QBLK * j), :] = jnp.where(lower, res[0][1], res[1][1])

    if dil == 1:
        residue(0)
    else:
        def body(r, carry):
            residue(r)
            return carry
        lax.fori_loop(0, dil, body, 0, unroll=2)


def _dilated_group(q, k, v, tb, gi, dil, tq):
    bsz, seq, _ = q.shape
    sub = seq // dil
    tq = min(tq, sub)
    tp = tq * dil
    halo = RADIUS * dil
    hb = tp // halo
    lane0 = gi * PAIRS_PER_GROUP
    cur = lambda p: pl.BlockSpec((1, tp, PAIR_WIDTH), lambda b, i: (b, i, lane0 + p))
    prev = lambda p: pl.BlockSpec((1, halo, PAIR_WIDTH), lambda b, i: (b, jnp.maximum(i * hb - 1, 0), lane0 + p))
    nxt = lambda p: pl.BlockSpec((1, halo, PAIR_WIDTH),
                                 lambda b, i: (b, jnp.minimum((i + 1) * hb, seq // halo - 1), lane0 + p))
    out = pl.BlockSpec((1, tp, PAIR_WIDTH), lambda b, i: (b, i, 0))
    pairs = range(PAIRS_PER_GROUP)
    halo_specs = [s(p) for p in pairs for s in (prev, cur, nxt)]
    res = pl.pallas_call(
        functools.partial(_attn_kernel, tq=tq, sub=sub, dil=dil),
        name="attn",
        grid=(bsz, seq // tp),
        in_specs=[cur(p) for p in pairs] + halo_specs + halo_specs
                 + [pl.BlockSpec((1, HEADS_PER_GROUP, QBLK, 2 * QBLK), lambda b, i: (gi, 0, 0, 0))],
        out_specs=[out] * (2 * PAIRS_PER_GROUP),
        out_shape=[jax.ShapeDtypeStruct((bsz, seq, PAIR_WIDTH), F32)] * (2 * PAIRS_PER_GROUP),
        compiler_params=_cparams("parallel", "parallel"),
    )(*([q] * PAIRS_PER_GROUP + [k] * (3 * PAIRS_PER_GROUP) + [v] * (3 * PAIRS_PER_GROUP) + [tb]))
    return res[:PAIRS_PER_GROUP], res[PAIRS_PER_GROUP:]


def _merge_kernel(*refs):
    ng = len(DILATED_GROUPS)
    npair = PAIRS_PER_GROUP
    x_ref, conv_ref, w_ref, x0_ref = refs[0:4]
    o_refs = refs[4:4 + ng * npair]
    l_refs = refs[4 + ng * npair:4 + 2 * ng * npair]
    g_ref, mod_ref, d_ref, g2_ref, whb_ref, wab_ref, wo_ref, wup_ref, wdn_ref, out_ref = refs[4 + 2 * ng * npair:]
    mod = mod_ref[0]
    gt1 = mod[:, 2 * D_MODEL:3 * D_MODEL]
    sh2 = mod[:, 3 * D_MODEL:4 * D_MODEL]
    sc2 = mod[:, 4 * D_MODEL:5 * D_MODEL]
    gt2 = mod[:, 5 * D_MODEL:6 * D_MODEL]

    merged = []
    for p in range(npair):
        lses = [l_refs[g * npair + p][0] for g in range(ng)]
        lm = functools.reduce(jnp.maximum, lses)
        es = [jnp.exp(l - lm) for l in lses]
        num = sum(e * o_refs[g * npair + p][0] for g, e in enumerate(es))
        merged.append(num / sum(es))
    y_at = jnp.concatenate(merged, axis=-1)

    y_hy = x0_ref[0] * (conv_ref[0] + w_ref[0] * d_ref[...])
    g = g_ref[0]
    mix = g[:, :D_MODEL] * _dot(y_hy.astype(BF16), whb_ref[...]) \
        + g[:, D_MODEL:] * _dot(y_at.astype(BF16), wab_ref[...])
    h = x_ref[0] + gt1 * _dot(mix.astype(BF16), wo_ref[...])

    ms = jnp.mean(h * h, axis=-1, keepdims=True)
    u2 = (h * lax.rsqrt(ms + EPS)) * (g2_ref[...] * (1.0 + sc2)) + sh2
    up = jnp.maximum(_dot(u2.astype(BF16), wup_ref[...]), 0.0)
    ff = _dot((up * up).astype(BF16), wdn_ref[...])
    out_ref[0] = h + gt2 * ff


def _merge_mlp(x, conv, w, x0, outs, lses, gates, mod, hyena_d, norm2_g, w_hy_br, w_at_br, w_out, w_up, w_down,
               tm):
    bsz, seq, _ = x.shape
    tile = lambda w: pl.BlockSpec((1, tm, w), lambda b, i: (b, i, 0))
    n_attn = len(outs) + len(lses)
    return pl.pallas_call(
        _merge_kernel,
        name="merge_mlp",
        grid=(bsz, seq // tm),
        in_specs=[tile(D_MODEL)] + [tile(D_HYENA)] * 3 + [tile(PAIR_WIDTH)] * n_attn + [tile(2 * D_MODEL),
                  pl.BlockSpec((1, 1, 6 * D_MODEL), lambda b, i: (b, 0, 0)),
                  _const_spec((1, D_HYENA)),
                  _const_spec((1, D_MODEL)),
                  _const_spec((D_HYENA, D_MODEL)),
                  _const_spec((PAIRS_PER_GROUP * PAIR_WIDTH, D_MODEL)),
                  _const_spec((D_MODEL, D_MODEL)),
                  _const_spec((D_MODEL, D_FF)),
                  _const_spec((D_FF, D_MODEL))],
        out_specs=tile(D_MODEL),
        out_shape=jax.ShapeDtypeStruct(x.shape, F32),
        compiler_params=_cparams("parallel", "parallel"),
    )(x, conv, w, x0, *outs, *lses, gates, mod, hyena_d.reshape(1, -1), norm2_g.reshape(1, -1), w_hy_br, w_at_br,
      w_out, w_up, w_down)


ATTN_TQ = {1: 512, 4: 256, 16: 128}


def _long_conv(w, kspec, consts):
    a = _dft_stage1(w, consts, is_filter=False)
    qf = _dft_stage2(a, kspec, consts["g2p"], consts["g2ip"], 5)
    return _dft_stage3(qf, consts)


def _filter_spec(seq, consts, filt):
    taps, asum = _filter_taps(seq, *filt)
    af = _dft_stage1(taps[None], consts, is_filter=True)
    return _filter_spectrum(af, consts["g2"], asum, 5)


def _encoder_layer(x, mod, tb, p):
    bsz, seq, _ = x.shape
    consts = {k: jnp.asarray(v) for k, v in _dft_consts(seq).items()}
    mod3 = mod.reshape(bsz, 1, 6 * D_MODEL)
    w, x0, q, k, v, gates = _in_projection(x, mod3, p["norm1_g"], p["w_in"], p["conv_w"], p["conv_b"],
                                           p["q_norm_g"], p["k_norm_g"], tm=256)
    kspec = _filter_spec(seq, consts, p["filt"])
    conv = _long_conv(w, kspec, consts)
    outs, lses = [], []
    for gi, (_, dil) in enumerate(DILATED_GROUPS):
        o, lse = _dilated_group(q, k, v, tb, gi, dil, ATTN_TQ[dil])
        outs.extend(o)
        lses.extend(lse)
    return _merge_mlp(x, conv, w, x0, outs, lses, gates, mod3, p["hyena_d"], p["norm2_g"], p["w_hy_br"],
                      p["w_at_br"], p["w_out"], p["w_up"], p["w_down"], tm=256)


def kernel(x_prompt, x_sample, c_prompt, c_sample, rel_bias, ada_w, ada_b, norm1_g, w_in, conv_w, conv_b,
           filt_w1, filt_b1, filt_w2, filt_b2, filt_w3, filt_b3, filt_freq, filt_w_out, hyena_d, q_norm_g,
           k_norm_g, w_hy_br, w_at_br, w_out, norm2_g, w_up, w_down):
    depth = ada_w.shape[0]
    tb = _bias_tables(rel_bias)
    y_prompt, y_sample = x_prompt, x_sample
    nbp = c_prompt.shape[0]
    for l in range(depth):
        p = dict(norm1_g=norm1_g[l], w_in=w_in[l].astype(BF16), conv_w=conv_w[l], conv_b=conv_b[l],
                 filt=(filt_w1[l], filt_b1[l], filt_w2[l], filt_b2[l], filt_w3[l], filt_b3[l], filt_freq[l],
                       filt_w_out[l]),
                 hyena_d=hyena_d[l], q_norm_g=q_norm_g[l], k_norm_g=k_norm_g[l],
                 w_hy_br=w_hy_br[l].astype(BF16), w_at_br=w_at_br[l].astype(BF16), w_out=w_out[l].astype(BF16),
                 norm2_g=norm2_g[l], w_up=w_up[l].astype(BF16), w_down=w_down[l].astype(BF16))
        mod = _modulation(jnp.concatenate([c_prompt, c_sample], axis=0), ada_w[l], ada_b[l])
        y_prompt = _encoder_layer(y_prompt, mod[:nbp], tb, p)
        y_sample = _encoder_layer(y_sample, mod[nbp:], tb, p)
    return (y_prompt, y_sample)
```

```python
import functools
import math

import numpy as np
import jax
import jax.numpy as jnp
from jax import lax
from jax.experimental import pallas as pl
from jax.experimental.pallas import tpu as pltpu

F32 = jnp.float32
BF16 = jnp.bfloat16
HIGHEST = lax.Precision.HIGHEST

D_MODEL = 1024
EPS = 1e-6
HEAD_DIM = 64
N_HEADS = 12
DILATED_GROUPS = ((128, 1), (512, 4), (2048, 16))
HEADS_PER_GROUP = 4
ATTN_WIDTH = 768
PAIR_WIDTH = 2 * HEAD_DIM
PAIRS_PER_GROUP = HEADS_PER_GROUP // 2
RADIUS = 64
NUM_BUCKETS = 32
MAX_DISTANCE = 1024
NEG_INF = -1e30
D_HYENA = 768
FILTER_BANDS = 16
FILTER_HIDDEN = 64
DECAY_TARGET = 1e-2
FAST_DECAY_PCT = 0.3
SLOW_DECAY_PCT = 1.5
D_FF = 4096
IN_PROJ_WIDTH = 6656

SUBLANES = 8
DFT_N1 = 128
DFT_H1 = 64
DFT_KP = 65
DFT_ROWS = DFT_KP * SUBLANES
QBLK = 128
VMEM_LIMIT = 56 * 1024 * 1024


def _cparams(*sem):
    return pltpu.CompilerParams(dimension_semantics=sem, vmem_limit_bytes=VMEM_LIMIT)


def _const_spec(shape):
    nd = len(shape)
    return pl.BlockSpec(shape, lambda *_: (0,) * nd, pipeline_mode=pl.Buffered(1))


def _dot(a, b, precision=None):
    return jnp.dot(a, b, preferred_element_type=F32, precision=precision)


@functools.lru_cache(maxsize=None)
def _dft_consts(L):
    n2n = L // DFT_H1
    n = 2 * L
    nch = n2n // SUBLANES
    r = np.arange(SUBLANES)
    eye = np.eye(SUBLANES)
    k1 = np.arange(DFT_KP)
    n1 = np.arange(DFT_H1)
    ang = -2.0 * np.pi * (r[:, None, None] * k1[None, :, None] / n + k1[None, :, None] * n1[None, None, :] / DFT_N1)
    f1 = np.stack([np.cos(ang), np.sin(ang)], axis=1)
    f0 = np.einsum("rpkn,rs->pkrns", f1, eye).reshape(2 * DFT_ROWS, DFT_H1 * SUBLANES)
    angc = 2.0 * np.pi * (n1[None, :, None] * k1[None, None, :] / DFT_N1 + r[:, None, None] * k1[None, None, :] / n)
    mult = np.full(DFT_KP, 2.0)
    mult[0] = mult[-1] = 1.0
    mi = np.stack([mult * np.cos(angc), -mult * np.sin(angc)], axis=2) / n
    m0 = np.einsum("rnpk,rs->nrpks", mi, eye).reshape(DFT_H1 * SUBLANES, 2 * DFT_ROWS)
    theta = -2.0 * np.pi * SUBLANES * np.arange(nch)[:, None] * k1[None, :] / n
    rep = lambda a: np.repeat(a, SUBLANES, axis=1)[:, :, None].astype(np.float32)
    sign = np.repeat(np.where(k1 % 2 == 0, 1.0, -1.0), SUBLANES)[:, None].astype(np.float32)
    a2 = -2.0 * np.pi * np.arange(n2n)[:, None] * np.arange(n2n)[None, :] / n2n
    fr, fi = np.cos(a2), np.sin(a2)
    g2 = np.block([[fr, -fi], [fi, fr]])
    g2i = np.block([[fr, fi], [-fi, fr]])
    as_bf16 = lambda a: np.asarray(a, np.float32).astype(BF16)
    f0p = np.einsum("rpkn,rs->kprns", f1, eye).reshape(2 * DFT_ROWS, DFT_H1 * SUBLANES)
    m0p = np.einsum("rnpk,rs->nrkps", mi, eye).reshape(DFT_H1 * SUBLANES, 2 * DFT_ROWS)
    jj, pp, rr = np.meshgrid(np.arange(nch), np.arange(2), r, indexing="ij")
    perm = (pp * n2n + SUBLANES * jj + rr).reshape(-1)
    per_k1 = lambda a: a[:, :, None, None].astype(np.float32)
    return dict(f0=as_bf16(f0), m0p=as_bf16(m0p), f0p=as_bf16(f0p), rot_c=rep(np.cos(theta)),
                rot_s=rep(np.sin(theta)), rotp_c=per_k1(np.cos(theta)), rotp_s=per_k1(np.sin(theta)), sign=sign,
                g2=as_bf16(g2), g2p=as_bf16(g2[:, perm]), g2ip=as_bf16(g2i[perm, :]))


def _t5_bucket_np(rel):
    half = NUM_BUCKETS // 2
    max_exact = half // 2
    n = np.abs(rel)
    ret = np.where(rel > 0, half, 0)
    large = max_exact + (np.log(np.maximum(n, 1).astype(np.float32) / np.float32(max_exact))
                         / np.float32(math.log(MAX_DISTANCE / max_exact))
                         * np.float32(half - max_exact)).astype(np.int32)
    large = np.minimum(large, half - 1)
    return ret + np.where(n < max_exact, n, large)


@functools.lru_cache(maxsize=None)
def _bucket_tables():
    qi = np.arange(QBLK)[:, None]
    c = np.arange(2 * QBLK)[None, :]
    rel = c - RADIUS - qi
    tabs = []
    for _, dil in DILATED_GROUPS:
        b = _t5_bucket_np(rel * dil)
        tabs.append(np.where(np.abs(rel) <= RADIUS, b, -1))
    return np.stack(tabs).astype(np.int32)


@functools.lru_cache(maxsize=None)
def _small_consts():
    bands = np.linspace(1e-4, FILTER_BANDS - 1, FILTER_BANDS, dtype=np.float32)[None, :]
    deltas = np.abs(np.linspace(math.log(DECAY_TARGET) / SLOW_DECAY_PCT,
                                math.log(DECAY_TARGET) / FAST_DECAY_PCT, D_HYENA, dtype=np.float32))[None, :]
    head = np.arange(ATTN_WIDTH) // HEAD_DIM
    blockdiag = (head[:, None] == head[None, :]).astype(np.float32) / HEAD_DIM
    return bands, deltas, blockdiag.astype(BF16)


def _mod_kernel(c_ref, w_ref, b_ref, o_ref):
    c = c_ref[...]
    s = c / (1.0 + jnp.exp(-c))
    o_ref[...] = _dot(s, w_ref[...], HIGHEST) + b_ref[...]


def _modulation(c, ada_w, ada_b):
    nb, _ = c.shape
    nw = ada_w.shape[1]
    tn = 1024
    return pl.pallas_call(
        _mod_kernel,
        name="mod",
        grid=(nw // tn,),
        in_specs=[pl.BlockSpec((nb, D_MODEL), lambda j: (0, 0)),
                  pl.BlockSpec((D_MODEL, tn), lambda j: (0, j)),
                  pl.BlockSpec((1, tn), lambda j: (0, j))],
        out_specs=pl.BlockSpec((nb, tn), lambda j: (0, j)),
        out_shape=jax.ShapeDtypeStruct((nb, nw), F32),
        compiler_params=_cparams("parallel"),
    )(c, ada_w, ada_b.reshape(1, nw))


def _bias_kernel(rb_ref, bk_ref, o_ref):
    g = pl.program_id(0)
    bk = bk_ref[0]
    for h in range(HEADS_PER_GROUP):
        acc = jnp.full(bk.shape, NEG_INF, F32)
        for b in range(NUM_BUCKETS):
            acc = jnp.where(bk == b, rb_ref[b, g * HEADS_PER_GROUP + h], acc)
        o_ref[0, h] = acc


def _bias_tables(rel_bias):
    bk = jnp.asarray(_bucket_tables())
    ng = len(DILATED_GROUPS)
    return pl.pallas_call(
        _bias_kernel,
        name="bias_tab",
        grid=(ng,),
        in_specs=[pl.BlockSpec(memory_space=pltpu.SMEM),
                  pl.BlockSpec((1, QBLK, 2 * QBLK), lambda g: (g, 0, 0))],
        out_specs=pl.BlockSpec((1, HEADS_PER_GROUP, QBLK, 2 * QBLK), lambda g: (g, 0, 0, 0)),
        out_shape=jax.ShapeDtypeStruct((ng, HEADS_PER_GROUP, QBLK, 2 * QBLK), F32),
        compiler_params=_cparams("arbitrary"),
    )(rel_bias, bk)


def _filter_kernel(w1t_ref, w1c_ref, w1s_ref, b1_ref, w2_ref, b2_ref, w3_ref, b3_ref, fr_ref, wo_ref,
                   bands_ref, deltas_ref, k_ref, asum_ref, *, seq, rows):
    i = pl.program_id(0)

    def lag(shape, axis):
        m = i * rows + lax.broadcasted_iota(jnp.int32, shape, axis)
        return m, jnp.where(m < seq, m, 2 * seq - m).astype(F32)

    _, pos_l = lag((1, rows), 1)
    t_l = pos_l / float(seq - 1)
    arg = bands_ref[...] * ((2.0 * math.pi) * pos_l / float(seq))
    fr = fr_ref[...]
    z = w1t_ref[...] * t_l + _dot(w1c_ref[...], jnp.cos(arg), HIGHEST) \
        + _dot(w1s_ref[...], -jnp.sin(arg), HIGHEST) + b1_ref[...]
    h = jnp.sin(fr * z)
    h = jnp.sin(fr * (_dot(w2_ref[...], h, HIGHEST) + b2_ref[...]))
    h = jnp.sin(fr * (_dot(w3_ref[...], h, HIGHEST) + b3_ref[...]))
    m, pos = lag((rows, 1), 0)
    t = pos / float(seq - 1)
    k = _dot(h.T, wo_ref[...], HIGHEST) * jnp.exp(-t * deltas_ref[...])
    k = jnp.where(m == seq, 0.0, k)
    k_ref[...] = k

    @pl.when(i == 0)
    def _():
        asum_ref[...] = jnp.zeros_like(asum_ref)

    asum_ref[...] += jnp.sum(jnp.abs(k), axis=0, keepdims=True)


def _filter_taps(seq, w1, b1, w2, b2, w3, b3, freq, w_out):
    bands, deltas, _ = _small_consts()
    rows = 1024
    nblk = 2 * seq // rows
    half = nblk // 2
    hid = FILTER_HIDDEN
    col = lambda a: a.reshape(-1, 1)
    small = lambda shape: pl.BlockSpec(shape, lambda i: (0, 0))
    return pl.pallas_call(
        functools.partial(_filter_kernel, seq=seq, rows=rows),
        name="filter_taps",
        grid=(nblk,),
        in_specs=[small((hid, 1)), small((hid, FILTER_BANDS)), small((hid, FILTER_BANDS)), small((hid, 1)),
                  small((hid, hid)), small((hid, 1)), small((hid, hid)), small((hid, 1)), small((hid, 1)),
                  pl.BlockSpec((hid, D_HYENA), lambda i: (0, jnp.where(i >= half, 1, 0))),
                  small((FILTER_BANDS, 1)), small((1, D_HYENA))],
        out_specs=[pl.BlockSpec((rows, D_HYENA), lambda i: (i, 0)),
                   pl.BlockSpec((1, D_HYENA), lambda i: (0, 0))],
        out_shape=[jax.ShapeDtypeStruct((2 * seq, D_HYENA), F32),
                   jax.ShapeDtypeStruct((1, D_HYENA), F32)],
        compiler_params=_cparams("arbitrary"),
    )(w1[0:1].T, w1[1:1 + FILTER_BANDS].T, w1[1 + FILTER_BANDS:].T, col(b1), w2.T, col(b2), w3.T, col(b3),
      col(freq), w_out, jnp.asarray(bands.T), jnp.asarray(deltas))


def _rotate(re, im, c, s):
    return c * re - s * im, s * re + c * im


def _store_spectrum_block(o_ref, re, im):
    c = re.shape[-1]
    o_ref[0, 0] = re.reshape(DFT_KP, SUBLANES, c)
    o_ref[0, 1] = im.reshape(DFT_KP, SUBLANES, c)


def _dft1_kernel(x_ref, f0_ref, rc_ref, rs_ref, o_ref):
    c = x_ref.shape[-1]
    x = x_ref[0].reshape(DFT_H1 * SUBLANES, c).astype(BF16)
    a = _dot(f0_ref[...], x).reshape(DFT_KP, 2, SUBLANES, c)
    re, im = _rotate(a[:, 0], a[:, 1], rc_ref[0], rs_ref[0])
    o_ref[0, :, 0] = jnp.concatenate([re, im], axis=1).astype(BF16)


def _dft1_filter_kernel(lo_ref, hi_ref, f0_ref, rc_ref, rs_ref, sign_ref, o_ref):
    c = lo_ref.shape[-1]
    flat = lambda ref: ref[0].reshape(DFT_H1 * SUBLANES, c).astype(BF16)
    lo = _dot(f0_ref[...], flat(lo_ref))
    hi = _dot(f0_ref[...], flat(hi_ref))
    sign = sign_ref[...]
    re, im = _rotate(lo[:DFT_ROWS] + sign * hi[:DFT_ROWS], lo[DFT_ROWS:] + sign * hi[DFT_ROWS:],
                     rc_ref[0], rs_ref[0])
    _store_spectrum_block(o_ref, re, im)


def _dft_stage1(x, consts, is_filter):
    bsz, rows, c = x.shape
    n2n = rows // (DFT_N1 if is_filter else DFT_H1)
    nch = n2n // SUBLANES
    rot = pl.BlockSpec((1, DFT_ROWS, 1), lambda j, b: (j, 0, 0))
    f0 = _const_spec((2 * DFT_ROWS, DFT_H1 * SUBLANES))
    out_spec = pl.BlockSpec((1, 2, DFT_KP, SUBLANES, c), lambda j, b: (b, 0, 0, j, 0))
    out_shape = jax.ShapeDtypeStruct((bsz, 2, DFT_KP, n2n, c), F32)
    if is_filter:
        xv = x.reshape(2, DFT_H1, n2n, c)
        data = [pl.BlockSpec((1, DFT_H1, SUBLANES, c), lambda j, b: (0, 0, j, 0)),
                pl.BlockSpec((1, DFT_H1, SUBLANES, c), lambda j, b: (1, 0, j, 0))]
        args = (xv, xv, consts["f0"], consts["rot_c"], consts["rot_s"], consts["sign"])
        specs = data + [f0, rot, rot, _const_spec((DFT_ROWS, 1))]
        body = _dft1_filter_kernel
    else:
        xv = x.reshape(bsz, DFT_H1, n2n, c)
        rotp = pl.BlockSpec((1, DFT_KP, 1, 1), lambda j, b: (j, 0, 0, 0))
        args = (xv, consts["f0p"], consts["rotp_c"], consts["rotp_s"])
        specs = [pl.BlockSpec((1, DFT_H1, SUBLANES, c), lambda j, b: (b, 0, j, 0)), f0, rotp, rotp]
        body = _dft1_kernel
        out_spec = pl.BlockSpec((1, DFT_KP, 1, 2 * SUBLANES, c), lambda j, b: (b, 0, j, 0, 0))
        out_shape = jax.ShapeDtypeStruct((bsz, DFT_KP, nch, 2 * SUBLANES, c), BF16)
    return pl.pallas_call(
        body,
        name="dft1",
        grid=(nch, bsz),
        in_specs=specs,
        out_specs=out_spec,
        out_shape=out_shape,
        compiler_params=_cparams("parallel", "parallel"),
    )(*args)


def _filter_spec_kernel(a_ref, g2_ref, asum_ref, o_ref, *, kb, n2n):
    scale = 1.0 / asum_ref[...]
    for j in range(kb):
        a = jnp.concatenate([a_ref[0, 0, j], a_ref[0, 1, j]], axis=0).astype(BF16)
        b = _dot(g2_ref[...], a) * scale
        o_ref[0, j] = b[:n2n]
        o_ref[1, j] = b[n2n:]


def _filter_spectrum(af, g2, asum, kb):
    _, _, kp, n2n, c = af.shape
    return pl.pallas_call(
        functools.partial(_filter_spec_kernel, kb=kb, n2n=n2n),
        name="filter_spec",
        grid=(kp // kb,),
        in_specs=[pl.BlockSpec((1, 2, kb, n2n, c), lambda i: (0, 0, i, 0, 0)),
                  pl.BlockSpec((2 * n2n, 2 * n2n), lambda i: (0, 0)),
                  pl.BlockSpec((1, c), lambda i: (0, 0))],
        out_specs=pl.BlockSpec((2, kb, n2n, c), lambda i: (0, i, 0, 0)),
        out_shape=jax.ShapeDtypeStruct((2, kp, n2n, c), F32),
        compiler_params=_cparams("parallel"),
    )(af, g2, asum)


def _dft2_kernel(a_ref, ks_ref, g2_ref, g2i_ref, o_ref, *, kb, n2n):
    nch, rows, c = a_ref.shape[2:]
    for j in range(kb):
        a = a_ref[0, j].reshape(2 * n2n, c)
        b = _dot(g2_ref[...], a)
        br, bi = b[:n2n], b[n2n:]
        kr, ki = ks_ref[0, j], ks_ref[1, j]
        p = jnp.concatenate([br * kr - bi * ki, br * ki + bi * kr], axis=0).astype(BF16)
        q = _dot(g2i_ref[...], p)
        o_ref[0, j] = q.astype(BF16).reshape(nch, rows, c)


def _dft_stage2(a, kspec, g2p, g2ip, kb):
    bsz, kp, nch, rows, c = a.shape
    n2n = nch * SUBLANES
    blk = pl.BlockSpec((1, kb, nch, rows, c), lambda i, b: (b, i, 0, 0, 0))
    return pl.pallas_call(
        functools.partial(_dft2_kernel, kb=kb, n2n=n2n),
        name="dft2",
        grid=(kp // kb, bsz),
        in_specs=[blk,
                  pl.BlockSpec((2, kb, n2n, c), lambda i, b: (0, i, 0, 0)),
                  pl.BlockSpec((2 * n2n, 2 * n2n), lambda i, b: (0, 0)),
                  pl.BlockSpec((2 * n2n, 2 * n2n), lambda i, b: (0, 0))],
        out_specs=blk,
        out_shape=jax.ShapeDtypeStruct(a.shape, BF16),
        compiler_params=_cparams("parallel", "parallel"),
    )(a, kspec, g2p, g2ip)


def _dft3_kernel(q_ref, m0_ref, rc_ref, rs_ref, y_ref):
    c = y_ref.shape[-1]
    q = q_ref[0, :, 0].astype(F32).reshape(DFT_KP, 2, SUBLANES, c)
    re, im = _rotate(q[:, 0], q[:, 1], rc_ref[0], -rs_ref[0])
    q = jnp.concatenate([re, im], axis=1).reshape(2 * DFT_ROWS, c).astype(BF16)
    y_ref[0] = _dot(m0_ref[...], q).reshape(DFT_H1, SUBLANES, c)


def _dft_stage3(q, consts):
    bsz, _, nch, rows, c = q.shape
    n2n = nch * SUBLANES
    rot = pl.BlockSpec((1, DFT_KP, 1, 1), lambda j, b: (j, 0, 0, 0))
    y = pl.pallas_call(
        _dft3_kernel,
        name="dft3",
        grid=(nch, bsz),
        in_specs=[pl.BlockSpec((1, DFT_KP, 1, rows, c), lambda j, b: (b, 0, j, 0, 0)),
                  _const_spec((DFT_H1 * SUBLANES, 2 * DFT_ROWS)), rot, rot],
        out_specs=pl.BlockSpec((1, DFT_H1, SUBLANES, c), lambda j, b: (b, 0, j, 0)),
        out_shape=jax.ShapeDtypeStruct((bsz, DFT_H1, n2n, c), F32),
        compiler_params=_cparams("parallel", "parallel"),
    )(q, consts["m0p"], consts["rotp_c"], consts["rotp_s"])
    return y.reshape(bsz, DFT_H1 * n2n, c)


def _inproj_kernel(x_ref, xp_ref, xn_ref, mod_ref, g1_ref, win_ref, cw_ref, cb_ref, gq_ref, gk_ref, bd_ref,
                   w_out, x0_out, q_out, k_out, v_out, g_out, *, tm, nt):
    i = pl.program_id(1)
    mod = mod_ref[0]
    sh1 = mod[:, 0:D_MODEL]
    scale1 = g1_ref[...] * (1.0 + mod[:, D_MODEL:2 * D_MODEL])

    def norm_mod(x):
        ms = jnp.mean(x * x, axis=-1, keepdims=True)
        return ((x * lax.rsqrt(ms + EPS)) * scale1 + sh1).astype(BF16)

    u = norm_mod(x_ref[0])
    uh = norm_mod(jnp.concatenate([xp_ref[0], xn_ref[0]], axis=0))
    ue = jnp.concatenate([u, uh], axis=0)
    row = lax.broadcasted_iota(jnp.int32, (tm, 1), 0)

    def conv_chunk(c):
        cols = slice(c * D_HYENA, (c + 1) * D_HYENA)
        ze = _dot(ue, win_ref[:, cols])
        z, zh = ze[:tm], ze[tm:]
        prev = jnp.where(i > 0, zh[7:8], 0.0)
        nxt = jnp.where(i < nt - 1, zh[8:9], 0.0)
        zm = jnp.where(row == 0, prev, pltpu.roll(z, 1, 0))
        zp = jnp.where(row == tm - 1, nxt, pltpu.roll(z, tm - 1, 0))
        return zm * cw_ref[0:1, cols] + z * cw_ref[1:2, cols] + zp * cw_ref[2:3, cols] + cb_ref[:, cols]

    x0_out[0] = conv_chunk(0)
    w_out[0] = conv_chunk(1) * conv_chunk(2)

    base = 3 * D_HYENA

    def head_norm(z, gain):
        ms = _dot((z * z).astype(BF16), bd_ref[...])
        return (z * lax.rsqrt(ms + EPS)) * gain

    zq = _dot(u, win_ref[:, base:base + ATTN_WIDTH])
    q_out[0] = head_norm(zq, gq_ref[...]) * (HEAD_DIM ** -0.5)
    zk = _dot(u, win_ref[:, base + ATTN_WIDTH:base + 2 * ATTN_WIDTH])
    k_out[0] = head_norm(zk, gk_ref[...])
    v_out[0] = _dot(u, win_ref[:, base + 2 * ATTN_WIDTH:base + 3 * ATTN_WIDTH])
    zg = _dot(u, win_ref[:, base + 3 * ATTN_WIDTH:])
    g_out[0] = 1.0 / (1.0 + jnp.exp(-zg))


def _in_projection(x, mod, norm1_g, w_in_bf16, conv_w, conv_b, q_norm_g, k_norm_g, tm):
    bsz, seq, _ = x.shape
    nt = seq // tm
    hb = tm // SUBLANES
    _, _, blockdiag = _small_consts()
    tile = lambda w: pl.BlockSpec((1, tm, w), lambda b, i: (b, i, 0))
    outs = [D_HYENA, D_HYENA, ATTN_WIDTH, ATTN_WIDTH, ATTN_WIDTH, 2 * D_MODEL]
    return pl.pallas_call(
        functools.partial(_inproj_kernel, tm=tm, nt=nt),
        name="inproj",
        grid=(bsz, nt),
        in_specs=[tile(D_MODEL),
                  pl.BlockSpec((1, SUBLANES, D_MODEL), lambda b, i: (b, jnp.maximum(i * hb - 1, 0), 0)),
                  pl.BlockSpec((1, SUBLANES, D_MODEL),
                               lambda b, i: (b, jnp.minimum((i + 1) * hb, seq // SUBLANES - 1), 0)),
                  pl.BlockSpec((1, 1, 6 * D_MODEL), lambda b, i: (b, 0, 0)),
                  _const_spec((1, D_MODEL)),
                  _const_spec((D_MODEL, IN_PROJ_WIDTH)),
                  _const_spec((3, 3 * D_HYENA)),
                  _const_spec((1, 3 * D_HYENA)),
                  _const_spec((1, ATTN_WIDTH)),
                  _const_spec((1, ATTN_WIDTH)),
                  _const_spec((ATTN_WIDTH, ATTN_WIDTH))],
        out_specs=[tile(w) for w in outs],
        out_shape=[jax.ShapeDtypeStruct((bsz, seq, w), F32) for w in outs],
        compiler_params=_cparams("parallel", "parallel"),
    )(x, x, x, mod, norm1_g.reshape(1, -1), w_in_bf16, conv_w, conv_b.reshape(1, -1),
      q_norm_g.reshape(1, -1), k_norm_g.reshape(1, -1), jnp.asarray(blockdiag))


def _attn_kernel(*refs, tq, sub, dil):
    npair = PAIRS_PER_GROUP
    q_refs = refs[0:npair]
    k_refs = refs[npair:4 * npair]
    v_refs = refs[4 * npair:7 * npair]
    tb_ref = refs[7 * npair]
    o_refs = refs[7 * npair + 1:8 * npair + 1]
    l_refs = refs[8 * npair + 1:9 * npair + 1]
    t0 = pl.program_id(1) * tq
    lower = lax.broadcasted_iota(jnp.int32, (1, PAIR_WIDTH), 1) < HEAD_DIM

    def rows(r, count, offset=0):
        if dil == 1:
            return pl.ds(offset, count)
        return pl.ds(r + offset * dil, count, stride=dil)

    def residue(r):
        for pair in range(npair):
            kp, kc, kn = k_refs[3 * pair:3 * pair + 3]
            vp, vc, vn = v_refs[3 * pair:3 * pair + 3]
            kfull = jnp.concatenate([kp[0, rows(r, RADIUS), :], kc[0, rows(r, tq), :], kn[0, rows(r, RADIUS), :]],
                                    axis=0).astype(BF16)
            vfull = jnp.concatenate([vp[0, rows(r, RADIUS), :], vc[0, rows(r, tq), :], vn[0, rows(r, RADIUS), :]],
                                    axis=0).astype(BF16)
            for j in range(tq // QBLK):
                kpos = t0 - RADIUS + QBLK * j + lax.broadcasted_iota(jnp.int32, (1, 2 * QBLK), 1)
                valid = (kpos >= 0) & (kpos < sub)
                win = slice(QBLK * j, QBLK * (j + 2))
                q2 = q_refs[pair][0, rows(r, QBLK, QBLK * j), :]
                k2 = kfull[win]
                v2 = vfull[win]
                res = []
                for hh in range(2):
                    sel = lower if hh == 0 else jnp.logical_not(lower)
                    qm = jnp.where(sel, q2, 0.0).astype(BF16)
                    s = lax.dot_general(qm, k2, (((1,), (1,)), ((), ())), preferred_element_type=F32)
                    s = jnp.where(valid, s + tb_ref[0, 2 * pair + hh], NEG_INF)
                    m = jnp.max(s, axis=-1, keepdims=True)
                    p = jnp.exp(s - m)
                    den = jnp.sum(p, axis=-1, keepdims=True)
                    res.append((_dot(p.astype(BF16), v2) / den, m + jnp.log(den)))
                o_refs[pair][0, rows(r, QBLK, QBLK * j), :] = jnp.where(lower, res[0][0], res[1][0])
                l_refs[pair][0, rows(r, QBLK, QBLK * j), :] = jnp.where(lower, res[0][1], res[1][1])

    if dil == 1:
        residue(0)
    else:
        def body(r, carry):
            residue(r)
            return carry
        lax.fori_loop(0, dil, body, 0, unroll=4)


def _dilated_group(q, k, v, tb, gi, dil, tq):
    bsz, seq, _ = q.shape
    sub = seq // dil
    tq = min(tq, sub)
    tp = tq * dil
    halo = RADIUS * dil
    hb = tp // halo
    lane0 = gi * PAIRS_PER_GROUP
    cur = lambda p: pl.BlockSpec((1, tp, PAIR_WIDTH), lambda b, i: (b, i, lane0 + p))
    prev = lambda p: pl.BlockSpec((1, halo, PAIR_WIDTH), lambda b, i: (b, jnp.maximum(i * hb - 1, 0), lane0 + p))
    nxt = lambda p: pl.BlockSpec((1, halo, PAIR_WIDTH),
                                 lambda b, i: (b, jnp.minimum((i + 1) * hb, seq // halo - 1), lane0 + p))
    out = pl.BlockSpec((1, tp, PAIR_WIDTH), lambda b, i: (b, i, 0))
    pairs = range(PAIRS_PER_GROUP)
    halo_specs = [s(p) for p in pairs for s in (prev, cur, nxt)]
    res = pl.pallas_call(
        functools.partial(_attn_kernel, tq=tq, sub=sub, dil=dil),
        name="attn",
        grid=(bsz, seq // tp),
        in_specs=[cur(p) for p in pairs] + halo_specs + halo_specs
                 + [pl.BlockSpec((1, HEADS_PER_GROUP, QBLK, 2 * QBLK), lambda b, i: (gi, 0, 0, 0))],
        out_specs=[out] * (2 * PAIRS_PER_GROUP),
        out_shape=[jax.ShapeDtypeStruct((bsz, seq, PAIR_WIDTH), F32)] * (2 * PAIRS_PER_GROUP),
        compiler_params=_cparams("parallel", "parallel"),
    )(*([q] * PAIRS_PER_GROUP + [k] * (3 * PAIRS_PER_GROUP) + [v] * (3 * PAIRS_PER_GROUP) + [tb]))
    return res[:PAIRS_PER_GROUP], res[PAIRS_PER_GROUP:]


def _merge_kernel(*refs):
    ng = len(DILATED_GROUPS)
    npair = PAIRS_PER_GROUP
    x_ref, conv_ref, w_ref, x0_ref = refs[0:4]
    o_refs = refs[4:4 + ng * npair]
    l_refs = refs[4 + ng * npair:4 + 2 * ng * npair]
    g_ref, mod_ref, d_ref, g2_ref, whb_ref, wab_ref, wo_ref, wup_ref, wdn_ref, out_ref = refs[4 + 2 * ng * npair:]
    mod = mod_ref[0]
    gt1 = mod[:, 2 * D_MODEL:3 * D_MODEL]
    sh2 = mod[:, 3 * D_MODEL:4 * D_MODEL]
    sc2 = mod[:, 4 * D_MODEL:5 * D_MODEL]
    gt2 = mod[:, 5 * D_MODEL:6 * D_MODEL]

    merged = []
    for p in range(npair):
        lses = [l_refs[g * npair + p][0] for g in range(ng)]
        lm = functools.reduce(jnp.maximum, lses)
        es = [jnp.exp(l - lm) for l in lses]
        num = sum(e * o_refs[g * npair + p][0] for g, e in enumerate(es))
        merged.append(num / sum(es))
    y_at = jnp.concatenate(merged, axis=-1)

    y_hy = x0_ref[0] * (conv_ref[0] + w_ref[0] * d_ref[...])
    g = g_ref[0]
    mix = g[:, :D_MODEL] * _dot(y_hy.astype(BF16), whb_ref[...]) \
        + g[:, D_MODEL:] * _dot(y_at.astype(BF16), wab_ref[...])
    h = x_ref[0] + gt1 * _dot(mix.astype(BF16), wo_ref[...])

    ms = jnp.mean(h * h, axis=-1, keepdims=True)
    u2 = (h * lax.rsqrt(ms + EPS)) * (g2_ref[...] * (1.0 + sc2)) + sh2
    up = jnp.maximum(_dot(u2.astype(BF16), wup_ref[...]), 0.0)
    ff = _dot((up * up).astype(BF16), wdn_ref[...])
    out_ref[0] = h + gt2 * ff


def _merge_mlp(x, conv, w, x0, outs, lses, gates, mod, hyena_d, norm2_g, w_hy_br, w_at_br, w_out, w_up, w_down,
               tm):
    bsz, seq, _ = x.shape
    tile = lambda w: pl.BlockSpec((1, tm, w), lambda b, i: (b, i, 0))
    n_attn = len(outs) + len(lses)
    return pl.pallas_call(
        _merge_kernel,
        name="merge_mlp",
        grid=(bsz, seq // tm),
        in_specs=[tile(D_MODEL)] + [tile(D_HYENA)] * 3 + [tile(PAIR_WIDTH)] * n_attn + [tile(2 * D_MODEL),
                  pl.BlockSpec((1, 1, 6 * D_MODEL), lambda b, i: (b, 0, 0)),
                  _const_spec((1, D_HYENA)),
                  _const_spec((1, D_MODEL)),
                  _const_spec((D_HYENA, D_MODEL)),
                  _const_spec((PAIRS_PER_GROUP * PAIR_WIDTH, D_MODEL)),
                  _const_spec((D_MODEL, D_MODEL)),
                  _const_spec((D_MODEL, D_FF)),
                  _const_spec((D_FF, D_MODEL))],
        out_specs=tile(D_MODEL),
        out_shape=jax.ShapeDtypeStruct(x.shape, F32),
        compiler_params=_cparams("parallel", "parallel"),
    )(x, conv, w, x0, *outs, *lses, gates, mod, hyena_d.reshape(1, -1), norm2_g.reshape(1, -1), w_hy_br, w_at_br,
      w_out, w_up, w_down)


ATTN_TQ = {1: 512, 4: 256, 16: 128}


def _long_conv(w, kspec, consts):
    a = _dft_stage1(w, consts, is_filter=False)
    qf = _dft_stage2(a, kspec, consts["g2p"], consts["g2ip"], 13)
    return _dft_stage3(qf, consts)


def _filter_spec(seq, consts, filt):
    taps, asum = _filter_taps(seq, *filt)
    af = _dft_stage1(taps[None], consts, is_filter=True)
    return _filter_spectrum(af, consts["g2"], asum, 5)


def _encoder_layer(x, mod, tb, p):
    bsz, seq, _ = x.shape
    consts = {k: jnp.asarray(v) for k, v in _dft_consts(seq).items()}
    mod3 = mod.reshape(bsz, 1, 6 * D_MODEL)
    w, x0, q, k, v, gates = _in_projection(x, mod3, p["norm1_g"], p["w_in"], p["conv_w"], p["conv_b"],
                                           p["q_norm_g"], p["k_norm_g"], tm=256)
    kspec = _filter_spec(seq, consts, p["filt"])
    conv = _long_conv(w, kspec, consts)
    outs, lses = [], []
    for gi, (_, dil) in enumerate(DILATED_GROUPS):
        o, lse = _dilated_group(q, k, v, tb, gi, dil, ATTN_TQ[dil])
        outs.extend(o)
        lses.extend(lse)
    return _merge_mlp(x, conv, w, x0, outs, lses, gates, mod3, p["hyena_d"], p["norm2_g"], p["w_hy_br"],
                      p["w_at_br"], p["w_out"], p["w_up"], p["w_down"], tm=256)


def kernel(x_prompt, x_sample, c_prompt, c_sample, rel_bias, ada_w, ada_b, norm1_g, w_in, conv_w, conv_b,
           filt_w1, filt_b1, filt_w2, filt_b2, filt_w3, filt_b3, filt_freq, filt_w_out, hyena_d, q_norm_g,
           k_norm_g, w_hy_br, w_at_br, w_out, norm2_g, w_up, w_down):
    depth = ada_w.shape[0]
    tb = _bias_tables(rel_bias)
    y_prompt, y_sample = x_prompt, x_sample
    nbp = c_prompt.shape[0]
    for l in range(depth):
        p = dict(norm1_g=norm1_g[l], w_in=w_in[l].astype(BF16), conv_w=conv_w[l], conv_b=conv_b[l],
                 filt=(filt_w1[l], filt_b1[l], filt_w2[l], filt_b2[l], filt_w3[l], filt_b3[l], filt_freq[l],
                       filt_w_out[l]),
                 hyena_d=hyena_d[l], q_norm_g=q_norm_g[l], k_norm_g=k_norm_g[l],
                 w_hy_br=w_hy_br[l].astype(BF16), w_at_br=w_at_br[l].astype(BF16), w_out=w_out[l].astype(BF16),
                 norm2_g=norm2_g[l], w_up=w_up[l].astype(BF16), w_down=w_down[l].astype(BF16))
        mod = _modulation(jnp.concatenate([c_prompt, c_sample], axis=0), ada_w[l], ada_b[l])
        y_prompt = _encoder_layer(y_prompt, mod[:nbp], tb, p)
        y_sample = _encoder_layer(y_sample, mod[nbp:], tb, p)
    return (y_prompt, y_sample)
```

```python
import functools
import math

import numpy as np
import jax
import jax.numpy as jnp
from jax import lax
from jax.experimental import pallas as pl
from jax.experimental.pallas import tpu as pltpu

F32 = jnp.float32
BF16 = jnp.bfloat16
HIGHEST = lax.Precision.HIGHEST

D_MODEL = 1024
EPS = 1e-6
HEAD_DIM = 64
N_HEADS = 12
DILATED_GROUPS = ((128, 1), (512, 4), (2048, 16))
HEADS_PER_GROUP = 4
ATTN_WIDTH = 768
PAIR_WIDTH = 2 * HEAD_DIM
PAIRS_PER_GROUP = HEADS_PER_GROUP // 2
RADIUS = 64
NUM_BUCKETS = 32
MAX_DISTANCE = 1024
NEG_INF = -1e30
D_HYENA = 768
FILTER_BANDS = 16
FILTER_HIDDEN = 64
DECAY_TARGET = 1e-2
FAST_DECAY_PCT = 0.3
SLOW_DECAY_PCT = 1.5
D_FF = 4096
IN_PROJ_WIDTH = 6656

SUBLANES = 8
DFT_N1 = 128
DFT_H1 = 64
DFT_KP = 65
DFT_ROWS = DFT_KP * SUBLANES
QBLK = 128
VMEM_LIMIT = 56 * 1024 * 1024


def _cparams(*sem):
    return pltpu.CompilerParams(dimension_semantics=sem, vmem_limit_bytes=VMEM_LIMIT)


def _const_spec(shape):
    nd = len(shape)
    return pl.BlockSpec(shape, lambda *_: (0,) * nd, pipeline_mode=pl.Buffered(1))


def _dot(a, b, precision=None):
    return jnp.dot(a, b, preferred_element_type=F32, precision=precision)


@functools.lru_cache(maxsize=None)
def _dft_consts(L):
    n2n = L // DFT_H1
    n = 2 * L
    nch = n2n // SUBLANES
    r = np.arange(SUBLANES)
    eye = np.eye(SUBLANES)
    k1 = np.arange(DFT_KP)
    n1 = np.arange(DFT_H1)
    ang = -2.0 * np.pi * (r[:, None, None] * k1[None, :, None] / n + k1[None, :, None] * n1[None, None, :] / DFT_N1)
    f1 = np.stack([np.cos(ang), np.sin(ang)], axis=1)
    f0 = np.einsum("rpkn,rs->pkrns", f1, eye).reshape(2 * DFT_ROWS, DFT_H1 * SUBLANES)
    angc = 2.0 * np.pi * (n1[None, :, None] * k1[None, None, :] / DFT_N1 + r[:, None, None] * k1[None, None, :] / n)
    mult = np.full(DFT_KP, 2.0)
    mult[0] = mult[-1] = 1.0
    mi = np.stack([mult * np.cos(angc), -mult * np.sin(angc)], axis=2) / n
    m0 = np.einsum("rnpk,rs->nrpks", mi, eye).reshape(DFT_H1 * SUBLANES, 2 * DFT_ROWS)
    theta = -2.0 * np.pi * SUBLANES * np.arange(nch)[:, None] * k1[None, :] / n
    rep = lambda a: np.repeat(a, SUBLANES, axis=1)[:, :, None].astype(np.float32)
    sign = np.repeat(np.where(k1 % 2 == 0, 1.0, -1.0), SUBLANES)[:, None].astype(np.float32)
    a2 = -2.0 * np.pi * np.arange(n2n)[:, None] * np.arange(n2n)[None, :] / n2n
    fr, fi = np.cos(a2), np.sin(a2)
    g2 = np.block([[fr, -fi], [fi, fr]])
    g2i = np.block([[fr, fi], [-fi, fr]])
    as_bf16 = lambda a: np.asarray(a, np.float32).astype(BF16)
    f0p = np.einsum("rpkn,rs->kprns", f1, eye).reshape(2 * DFT_ROWS, DFT_H1 * SUBLANES)
    m0p = np.einsum("rnpk,rs->nrkps", mi, eye).reshape(DFT_H1 * SUBLANES, 2 * DFT_ROWS)
    jj, pp, rr = np.meshgrid(np.arange(nch), np.arange(2), r, indexing="ij")
    perm = (pp * n2n + SUBLANES * jj + rr).reshape(-1)
    per_k1 = lambda a: a[:, :, None, None].astype(np.float32)
    return dict(f0=as_bf16(f0), m0p=as_bf16(m0p), f0p=as_bf16(f0p), rot_c=rep(np.cos(theta)),
                rot_s=rep(np.sin(theta)), rotp_c=per_k1(np.cos(theta)), rotp_s=per_k1(np.sin(theta)), sign=sign,
                g2=as_bf16(g2), g2p=as_bf16(g2[:, perm]), g2ip=as_bf16(g2i[perm, :]))


def _t5_bucket_np(rel):
    half = NUM_BUCKETS // 2
    max_exact = half // 2
    n = np.abs(rel)
    ret = np.where(rel > 0, half, 0)
    large = max_exact + (np.log(np.maximum(n, 1).astype(np.float32) / np.float32(max_exact))
                         / np.float32(math.log(MAX_DISTANCE / max_exact))
                         * np.float32(half - max_exact)).astype(np.int32)
    large = np.minimum(large, half - 1)
    return ret + np.where(n < max_exact, n, large)


@functools.lru_cache(maxsize=None)
def _bucket_tables():
    qi = np.arange(QBLK)[:, None]
    c = np.arange(2 * QBLK)[None, :]
    rel = c - RADIUS - qi
    tabs = []
    for _, dil in DILATED_GROUPS:
        b = _t5_bucket_np(rel * dil)
        tabs.append(np.where(np.abs(rel) <= RADIUS, b, -1))
    return np.stack(tabs).astype(np.int32)


@functools.lru_cache(maxsize=None)
def _small_consts():
    bands = np.linspace(1e-4, FILTER_BANDS - 1, FILTER_BANDS, dtype=np.float32)[None, :]
    deltas = np.abs(np.linspace(math.log(DECAY_TARGET) / SLOW_DECAY_PCT,
                                math.log(DECAY_TARGET) / FAST_DECAY_PCT, D_HYENA, dtype=np.float32))[None, :]
    head = np.arange(ATTN_WIDTH) // HEAD_DIM
    blockdiag = (head[:, None] == head[None, :]).astype(np.float32) / HEAD_DIM
    return bands, deltas, blockdiag.astype(BF16)


def _mod_kernel(c_ref, w_ref, b_ref, o_ref):
    c = c_ref[...]
    s = c / (1.0 + jnp.exp(-c))
    o_ref[...] = _dot(s, w_ref[...], HIGHEST) + b_ref[...]


def _modulation(c, ada_w, ada_b):
    nb, _ = c.shape
    nw = ada_w.shape[1]
    tn = 1024
    return pl.pallas_call(
        _mod_kernel,
        name="mod",
        grid=(nw // tn,),
        in_specs=[pl.BlockSpec((nb, D_MODEL), lambda j: (0, 0)),
                  pl.BlockSpec((D_MODEL, tn), lambda j: (0, j)),
                  pl.BlockSpec((1, tn), lambda j: (0, j))],
        out_specs=pl.BlockSpec((nb, tn), lambda j: (0, j)),
        out_shape=jax.ShapeDtypeStruct((nb, nw), F32),
        compiler_params=_cparams("parallel"),
    )(c, ada_w, ada_b.reshape(1, nw))


def _bias_kernel(rb_ref, bk_ref, o_ref):
    g = pl.program_id(0)
    bk = bk_ref[0]
    for h in range(HEADS_PER_GROUP):
        acc = jnp.full(bk.shape, NEG_INF, F32)
        for b in range(NUM_BUCKETS):
            acc = jnp.where(bk == b, rb_ref[b, g * HEADS_PER_GROUP + h], acc)
        o_ref[0, h] = acc


def _bias_tables(rel_bias):
    bk = jnp.asarray(_bucket_tables())
    ng = len(DILATED_GROUPS)
    return pl.pallas_call(
        _bias_kernel,
        name="bias_tab",
        grid=(ng,),
        in_specs=[pl.BlockSpec(memory_space=pltpu.SMEM),
                  pl.BlockSpec((1, QBLK, 2 * QBLK), lambda g: (g, 0, 0))],
        out_specs=pl.BlockSpec((1, HEADS_PER_GROUP, QBLK, 2 * QBLK), lambda g: (g, 0, 0, 0)),
        out_shape=jax.ShapeDtypeStruct((ng, HEADS_PER_GROUP, QBLK, 2 * QBLK), F32),
        compiler_params=_cparams("arbitrary"),
    )(rel_bias, bk)


def _filter_kernel(w1t_ref, w1c_ref, w1s_ref, b1_ref, w2_ref, b2_ref, w3_ref, b3_ref, fr_ref, wo_ref,
                   bands_ref, deltas_ref, k_ref, asum_ref, *, seq, rows):
    i = pl.program_id(0)

    def lag(shape, axis):
        m = i * rows + lax.broadcasted_iota(jnp.int32, shape, axis)
        return m, jnp.where(m < seq, m, 2 * seq - m).astype(F32)

    _, pos_l = lag((1, rows), 1)
    t_l = pos_l / float(seq - 1)
    arg = bands_ref[...] * ((2.0 * math.pi) * pos_l / float(seq))
    fr = fr_ref[...]
    z = w1t_ref[...] * t_l + _dot(w1c_ref[...], jnp.cos(arg), HIGHEST) \
        + _dot(w1s_ref[...], -jnp.sin(arg), HIGHEST) + b1_ref[...]
    h = jnp.sin(fr * z)
    h = jnp.sin(fr * (_dot(w2_ref[...], h, HIGHEST) + b2_ref[...]))
    h = jnp.sin(fr * (_dot(w3_ref[...], h, HIGHEST) + b3_ref[...]))
    m, pos = lag((rows, 1), 0)
    t = pos / float(seq - 1)
    k = _dot(h.T, wo_ref[...], HIGHEST) * jnp.exp(-t * deltas_ref[...])
    k = jnp.where(m == seq, 0.0, k)
    k_ref[...] = k

    @pl.when(i == 0)
    def _():
        asum_ref[...] = jnp.zeros_like(asum_ref)

    asum_ref[...] += jnp.sum(jnp.abs(k), axis=0, keepdims=True)


def _filter_taps(seq, w1, b1, w2, b2, w3, b3, freq, w_out):
    bands, deltas, _ = _small_consts()
    rows = 1024
    nblk = 2 * seq // rows
    half = nblk // 2
    hid = FILTER_HIDDEN
    col = lambda a: a.reshape(-1, 1)
    small = lambda shape: pl.BlockSpec(shape, lambda i: (0, 0))
    return pl.pallas_call(
        functools.partial(_filter_kernel, seq=seq, rows=rows),
        name="filter_taps",
        grid=(nblk,),
        in_specs=[small((hid, 1)), small((hid, FILTER_BANDS)), small((hid, FILTER_BANDS)), small((hid, 1)),
                  small((hid, hid)), small((hid, 1)), small((hid, hid)), small((hid, 1)), small((hid, 1)),
                  pl.BlockSpec((hid, D_HYENA), lambda i: (0, jnp.where(i >= half, 1, 0))),
                  small((FILTER_BANDS, 1)), small((1, D_HYENA))],
        out_specs=[pl.BlockSpec((rows, D_HYENA), lambda i: (i, 0)),
                   pl.BlockSpec((1, D_HYENA), lambda i: (0, 0))],
        out_shape=[jax.ShapeDtypeStruct((2 * seq, D_HYENA), F32),
                   jax.ShapeDtypeStruct((1, D_HYENA), F32)],
        compiler_params=_cparams("arbitrary"),
    )(w1[0:1].T, w1[1:1 + FILTER_BANDS].T, w1[1 + FILTER_BANDS:].T, col(b1), w2.T, col(b2), w3.T, col(b3),
      col(freq), w_out, jnp.asarray(bands.T), jnp.asarray(deltas))


def _rotate(re, im, c, s):
    return c * re - s * im, s * re + c * im


def _store_spectrum_block(o_ref, re, im):
    c = re.shape[-1]
    o_ref[0, 0] = re.reshape(DFT_KP, SUBLANES, c)
    o_ref[0, 1] = im.reshape(DFT_KP, SUBLANES, c)


def _dft1_kernel(x_ref, f0_ref, rc_ref, rs_ref, o_ref):
    c = x_ref.shape[-1]
    x = x_ref[0].reshape(DFT_H1 * SUBLANES, c).astype(BF16)
    a = _dot(f0_ref[...], x).reshape(DFT_KP, 2, SUBLANES, c)
    re, im = _rotate(a[:, 0], a[:, 1], rc_ref[0], rs_ref[0])
    o_ref[0, :, 0] = jnp.concatenate([re, im], axis=1).astype(BF16)


def _dft1_filter_kernel(lo_ref, hi_ref, f0_ref, rc_ref, rs_ref, sign_ref, o_ref):
    c = lo_ref.shape[-1]
    flat = lambda ref: ref[0].reshape(DFT_H1 * SUBLANES, c).astype(BF16)
    lo = _dot(f0_ref[...], flat(lo_ref))
    hi = _dot(f0_ref[...], flat(hi_ref))
    sign = sign_ref[...]
    re, im = _rotate(lo[:DFT_ROWS] + sign * hi[:DFT_ROWS], lo[DFT_ROWS:] + sign * hi[DFT_ROWS:],
                     rc_ref[0], rs_ref[0])
    _store_spectrum_block(o_ref, re, im)


def _dft_stage1(x, consts, is_filter):
    bsz, rows, c = x.shape
    n2n = rows // (DFT_N1 if is_filter else DFT_H1)
    nch = n2n // SUBLANES
    rot = pl.BlockSpec((1, DFT_ROWS, 1), lambda j, b: (j, 0, 0))
    f0 = _const_spec((2 * DFT_ROWS, DFT_H1 * SUBLANES))
    out_spec = pl.BlockSpec((1, 2, DFT_KP, SUBLANES, c), lambda j, b: (b, 0, 0, j, 0))
    out_shape = jax.ShapeDtypeStruct((bsz, 2, DFT_KP, n2n, c), F32)
    if is_filter:
        xv = x.reshape(2, DFT_H1, n2n, c)
        data = [pl.BlockSpec((1, DFT_H1, SUBLANES, c), lambda j, b: (0, 0, j, 0)),
                pl.BlockSpec((1, DFT_H1, SUBLANES, c), lambda j, b: (1, 0, j, 0))]
        args = (xv, xv, consts["f0"], consts["rot_c"], consts["rot_s"], consts["sign"])
        specs = data + [f0, rot, rot, _const_spec((DFT_ROWS, 1))]
        body = _dft1_filter_kernel
    else:
        xv = x.reshape(bsz, DFT_H1, n2n, c)
        rotp = pl.BlockSpec((1, DFT_KP, 1, 1), lambda j, b: (j, 0, 0, 0))
        args = (xv, consts["f0p"], consts["rotp_c"], consts["rotp_s"])
        specs = [pl.BlockSpec((1, DFT_H1, SUBLANES, c), lambda j, b: (b, 0, j, 0)), f0, rotp, rotp]
        body = _dft1_kernel
        out_spec = pl.BlockSpec((1, DFT_KP, 1, 2 * SUBLANES, c), lambda j, b: (b, 0, j, 0, 0))
        out_shape = jax.ShapeDtypeStruct((bsz, DFT_KP, nch, 2 * SUBLANES, c), BF16)
    return pl.pallas_call(
        body,
        name="dft1",
        grid=(nch, bsz),
        in_specs=specs,
        out_specs=out_spec,
        out_shape=out_shape,
        compiler_params=_cparams("parallel", "parallel"),
    )(*args)


def _filter_spec_kernel(a_ref, g2_ref, asum_ref, o_ref, *, kb, n2n):
    scale = 1.0 / asum_ref[...]
    for j in range(kb):
        a = jnp.concatenate([a_ref[0, 0, j], a_ref[0, 1, j]], axis=0).astype(BF16)
        b = _dot(g2_ref[...], a) * scale
        o_ref[0, j] = b[:n2n]
        o_ref[1, j] = b[n2n:]


def _filter_spectrum(af, g2, asum, kb):
    _, _, kp, n2n, c = af.shape
    return pl.pallas_call(
        functools.partial(_filter_spec_kernel, kb=kb, n2n=n2n),
        name="filter_spec",
        grid=(kp // kb,),
        in_specs=[pl.BlockSpec((1, 2, kb, n2n, c), lambda i: (0, 0, i, 0, 0)),
                  pl.BlockSpec((2 * n2n, 2 * n2n), lambda i: (0, 0)),
                  pl.BlockSpec((1, c), lambda i: (0, 0))],
        out_specs=pl.BlockSpec((2, kb, n2n, c), lambda i: (0, i, 0, 0)),
        out_shape=jax.ShapeDtypeStruct((2, kp, n2n, c), F32),
        compiler_params=_cparams("parallel"),
    )(af, g2, asum)


def _dft2_kernel(a_ref, ks_ref, g2_ref, g2i_ref, o_ref, *, kb, n2n):
    nch, rows, c = a_ref.shape[2:]
    for j in range(kb):
        a = a_ref[0, j].reshape(2 * n2n, c)
        b = _dot(g2_ref[...], a)
        br, bi = b[:n2n], b[n2n:]
        kr, ki = ks_ref[0, j], ks_ref[1, j]
        p = jnp.concatenate([br * kr - bi * ki, br * ki + bi * kr], axis=0).astype(BF16)
        q = _dot(g2i_ref[...], p)
        o_ref[0, j] = q.astype(BF16).reshape(nch, rows, c)


def _dft_stage2(a, kspec, g2p, g2ip, kb):
    bsz, kp, nch, rows, c = a.shape
    n2n = nch * SUBLANES
    blk = pl.BlockSpec((1, kb, nch, rows, c), lambda i, b: (b, i, 0, 0, 0))
    return pl.pallas_call(
        functools.partial(_dft2_kernel, kb=kb, n2n=n2n),
        name="dft2",
        grid=(kp // kb, bsz),
        in_specs=[blk,
                  pl.BlockSpec((2, kb, n2n, c), lambda i, b: (0, i, 0, 0)),
                  pl.BlockSpec((2 * n2n, 2 * n2n), lambda i, b: (0, 0)),
                  pl.BlockSpec((2 * n2n, 2 * n2n), lambda i, b: (0, 0))],
        out_specs=blk,
        out_shape=jax.ShapeDtypeStruct(a.shape, BF16),
        compiler_params=_cparams("parallel", "parallel"),
    )(a, kspec, g2p, g2ip)


def _dft3_kernel(q_ref, m0_ref, rc_ref, rs_ref, y_ref):
    c = y_ref.shape[-1]
    q = q_ref[0, :, 0].astype(F32).reshape(DFT_KP, 2, SUBLANES, c)
    re, im = _rotate(q[:, 0], q[:, 1], rc_ref[0], -rs_ref[0])
    q = jnp.concatenate([re, im], axis=1).reshape(2 * DFT_ROWS, c).astype(BF16)
    y_ref[0] = _dot(m0_ref[...], q).reshape(DFT_H1, SUBLANES, c)


def _dft_stage3(q, consts):
    bsz, _, nch, rows, c = q.shape
    n2n = nch * SUBLANES
    rot = pl.BlockSpec((1, DFT_KP, 1, 1), lambda j, b: (j, 0, 0, 0))
    y = pl.pallas_call(
        _dft3_kernel,
        name="dft3",
        grid=(nch, bsz),
        in_specs=[pl.BlockSpec((1, DFT_KP, 1, rows, c), lambda j, b: (b, 0, j, 0, 0)),
                  _const_spec((DFT_H1 * SUBLANES, 2 * DFT_ROWS)), rot, rot],
        out_specs=pl.BlockSpec((1, DFT_H1, SUBLANES, c), lambda j, b: (b, 0, j, 0)),
        out_shape=jax.ShapeDtypeStruct((bsz, DFT_H1, n2n, c), F32),
        compiler_params=_cparams("parallel", "parallel"),
    )(q, consts["m0p"], consts["rotp_c"], consts["rotp_s"])
    return y.reshape(bsz, DFT_H1 * n2n, c)


def _inproj_kernel(x_ref, xp_ref, xn_ref, mod_ref, g1_ref, win_ref, cw_ref, cb_ref, gq_ref, gk_ref, bd_ref,
                   w_out, x0_out, q_out, k_out, v_out, g_out, *, tm, nt):
    i = pl.program_id(1)
    mod = mod_ref[0]
    sh1 = mod[:, 0:D_MODEL]
    scale1 = g1_ref[...] * (1.0 + mod[:, D_MODEL:2 * D_MODEL])

    def norm_mod(x):
        ms = jnp.mean(x * x, axis=-1, keepdims=True)
        return ((x * lax.rsqrt(ms + EPS)) * scale1 + sh1).astype(BF16)

    u = norm_mod(x_ref[0])
    uh = norm_mod(jnp.concatenate([xp_ref[0], xn_ref[0]], axis=0))
    ue = jnp.concatenate([u, uh], axis=0)
    row = lax.broadcasted_iota(jnp.int32, (tm, 1), 0)

    def conv_chunk(c):
        cols = slice(c * D_HYENA, (c + 1) * D_HYENA)
        ze = _dot(ue, win_ref[:, cols])
        z, zh = ze[:tm], ze[tm:]
        prev = jnp.where(i > 0, zh[7:8], 0.0)
        nxt = jnp.where(i < nt - 1, zh[8:9], 0.0)
        zm = jnp.where(row == 0, prev, pltpu.roll(z, 1, 0))
        zp = jnp.where(row == tm - 1, nxt, pltpu.roll(z, tm - 1, 0))
        return zm * cw_ref[0:1, cols] + z * cw_ref[1:2, cols] + zp * cw_ref[2:3, cols] + cb_ref[:, cols]

    x0_out[0] = conv_chunk(0)
    w_out[0] = conv_chunk(1) * conv_chunk(2)

    base = 3 * D_HYENA

    def head_norm(z, gain):
        ms = _dot((z * z).astype(BF16), bd_ref[...])
        return (z * lax.rsqrt(ms + EPS)) * gain

    zq = _dot(u, win_ref[:, base:base + ATTN_WIDTH])
    q_out[0] = head_norm(zq, gq_ref[...]) * (HEAD_DIM ** -0.5)
    zk = _dot(u, win_ref[:, base + ATTN_WIDTH:base + 2 * ATTN_WIDTH])
    k_out[0] = head_norm(zk, gk_ref[...])
    v_out[0] = _dot(u, win_ref[:, base + 2 * ATTN_WIDTH:base + 3 * ATTN_WIDTH])
    zg = _dot(u, win_ref[:, base + 3 * ATTN_WIDTH:])
    g_out[0] = 1.0 / (1.0 + jnp.exp(-zg))


def _in_projection(x, mod, norm1_g, w_in_bf16, conv_w, conv_b, q_norm_g, k_norm_g, tm):
    bsz, seq, _ = x.shape
    nt = seq // tm
    hb = tm // SUBLANES
    _, _, blockdiag = _small_consts()
    tile = lambda w: pl.BlockSpec((1, tm, w), lambda b, i: (b, i, 0))
    outs = [D_HYENA, D_HYENA, ATTN_WIDTH, ATTN_WIDTH, ATTN_WIDTH, 2 * D_MODEL]
    return pl.pallas_call(
        functools.partial(_inproj_kernel, tm=tm, nt=nt),
        name="inproj",
        grid=(bsz, nt),
        in_specs=[tile(D_MODEL),
                  pl.BlockSpec((1, SUBLANES, D_MODEL), lambda b, i: (b, jnp.maximum(i * hb - 1, 0), 0)),
                  pl.BlockSpec((1, SUBLANES, D_MODEL),
                               lambda b, i: (b, jnp.minimum((i + 1) * hb, seq // SUBLANES - 1), 0)),
                  pl.BlockSpec((1, 1, 6 * D_MODEL), lambda b, i: (b, 0, 0)),
                  _const_spec((1, D_MODEL)),
                  _const_spec((D_MODEL, IN_PROJ_WIDTH)),
                  _const_spec((3, 3 * D_HYENA)),
                  _const_spec((1, 3 * D_HYENA)),
                  _const_spec((1, ATTN_WIDTH)),
                  _const_spec((1, ATTN_WIDTH)),
                  _const_spec((ATTN_WIDTH, ATTN_WIDTH))],
        out_specs=[tile(w) for w in outs],
        out_shape=[jax.ShapeDtypeStruct((bsz, seq, w), F32) for w in outs],
        compiler_params=_cparams("parallel", "parallel"),
    )(x, x, x, mod, norm1_g.reshape(1, -1), w_in_bf16, conv_w, conv_b.reshape(1, -1),
      q_norm_g.reshape(1, -1), k_norm_g.reshape(1, -1), jnp.asarray(blockdiag))


def _attn_kernel(*refs, tq, sub, dil):
    npair = PAIRS_PER_GROUP
    q_refs = refs[0:npair]
    k_refs = refs[npair:4 * npair]
    v_refs = refs[4 * npair:7 * npair]
    tb_ref = refs[7 * npair]
    o_refs = refs[7 * npair + 1:8 * npair + 1]
    l_refs = refs[8 * npair + 1:9 * npair + 1]
    t0 = pl.program_id(1) * tq
    lower = lax.broadcasted_iota(jnp.int32, (1, PAIR_WIDTH), 1) < HEAD_DIM

    def rows(r, count, offset=0):
        if dil == 1:
            return pl.ds(offset, count)
        return pl.ds(r + offset * dil, count, stride=dil)

    def residue(r):
        for pair in range(npair):
            kp, kc, kn = k_refs[3 * pair:3 * pair + 3]
            vp, vc, vn = v_refs[3 * pair:3 * pair + 3]
            kfull = jnp.concatenate([kp[0, rows(r, RADIUS), :], kc[0, rows(r, tq), :], kn[0, rows(r, RADIUS), :]],
                                    axis=0).astype(BF16)
            vfull = jnp.concatenate([vp[0, rows(r, RADIUS), :], vc[0, rows(r, tq), :], vn[0, rows(r, RADIUS), :]],
                                    axis=0).astype(BF16)
            for j in range(tq // QBLK):
                kpos = t0 - RADIUS + QBLK * j + lax.broadcasted_iota(jnp.int32, (1, 2 * QBLK), 1)
                valid = (kpos >= 0) & (kpos < sub)
                win = slice(QBLK * j, QBLK * (j + 2))
                q2 = q_refs[pair][0, rows(r, QBLK, QBLK * j), :]
                k2 = kfull[win]
                v2 = vfull[win]
                res = []
                for hh in range(2):
                    sel = lower if hh == 0 else jnp.logical_not(lower)
                    qm = jnp.where(sel, q2, 0.0).astype(BF16)
                    s = lax.dot_general(qm, k2, (((1,), (1,)), ((), ())), preferred_element_type=F32)
                    s = jnp.where(valid, s + tb_ref[0, 2 * pair + hh], NEG_INF)
                    m = jnp.max(s, axis=-1, keepdims=True)
                    p = jnp.exp(s - m)
                    den = jnp.sum(p, axis=-1, keepdims=True)
                    res.append((_dot(p.astype(BF16), v2) / den, m + jnp.log(den)))
                o_refs[pair][0, rows(r, QBLK, QBLK * j), :] = jnp.where(lower, res[0][0], res[1][0])
                l_refs[pair][0, rows(r, QBLK, QBLK * j), :] = jnp.where(lower, res[0][1], res[1][1])

    if dil == 1:
        residue(0)
    else:
        def body(r, carry):
            residue(r)
            return carry
        lax.fori_loop(0, dil, body, 0, unroll=4)


def _dilated_group(q, k, v, tb, gi, dil, tq):
    bsz, seq, _ = q.shape
    sub = seq // dil
    tq = min(tq, sub)
    tp = tq * dil
    halo = RADIUS * dil
    hb = tp // halo
    lane0 = gi * PAIRS_PER_GROUP
    cur = lambda p: pl.BlockSpec((1, tp, PAIR_WIDTH), lambda b, i: (b, i, lane0 + p))
    prev = lambda p: pl.BlockSpec((1, halo, PAIR_WIDTH), lambda b, i: (b, jnp.maximum(i * hb - 1, 0), lane0 + p))
    nxt = lambda p: pl.BlockSpec((1, halo, PAIR_WIDTH),
                                 lambda b, i: (b, jnp.minimum((i + 1) * hb, seq // halo - 1), lane0 + p))
    out = pl.BlockSpec((1, tp, PAIR_WIDTH), lambda b, i: (b, i, 0))
    pairs = range(PAIRS_PER_GROUP)
    halo_specs = [s(p) for p in pairs for s in (prev, cur, nxt)]
    res = pl.pallas_call(
        functools.partial(_attn_kernel, tq=tq, sub=sub, dil=dil),
        name="attn",
        grid=(bsz, seq // tp),
        in_specs=[cur(p) for p in pairs] + halo_specs + halo_specs
                 + [pl.BlockSpec((1, HEADS_PER_GROUP, QBLK, 2 * QBLK), lambda b, i: (gi, 0, 0, 0))],
        out_specs=[out] * (2 * PAIRS_PER_GROUP),
        out_shape=[jax.ShapeDtypeStruct((bsz, seq, PAIR_WIDTH), F32)] * (2 * PAIRS_PER_GROUP),
        compiler_params=_cparams("parallel", "parallel"),
    )(*([q] * PAIRS_PER_GROUP + [k] * (3 * PAIRS_PER_GROUP) + [v] * (3 * PAIRS_PER_GROUP) + [tb]))
    return res[:PAIRS_PER_GROUP], res[PAIRS_PER_GROUP:]


def _merge_kernel(*refs):
    ng = len(DILATED_GROUPS)
    npair = PAIRS_PER_GROUP
    x_ref, conv_ref, w_ref, x0_ref = refs[0:4]
    o_refs = refs[4:4 + ng * npair]
    l_refs = refs[4 + ng * npair:4 + 2 * ng * npair]
    g_ref, mod_ref, d_ref, g2_ref, whb_ref, wab_ref, wo_ref, wup_ref, wdn_ref, out_ref = refs[4 + 2 * ng * npair:]
    mod = mod_ref[0]
    gt1 = mod[:, 2 * D_MODEL:3 * D_MODEL]
    sh2 = mod[:, 3 * D_MODEL:4 * D_MODEL]
    sc2 = mod[:, 4 * D_MODEL:5 * D_MODEL]
    gt2 = mod[:, 5 * D_MODEL:6 * D_MODEL]

    merged = []
    for p in range(npair):
        lses = [l_refs[g * npair + p][0] for g in range(ng)]
        lm = functools.reduce(jnp.maximum, lses)
        es = [jnp.exp(l - lm) for l in lses]
        num = sum(e * o_refs[g * npair + p][0] for g, e in enumerate(es))
        merged.append(num / sum(es))
    y_at = jnp.concatenate(merged, axis=-1)

    y_hy = x0_ref[0] * (conv_ref[0] + w_ref[0] * d_ref[...])
    g = g_ref[0]
    mix = g[:, :D_MODEL] * _dot(y_hy.astype(BF16), whb_ref[...]) \
        + g[:, D_MODEL:] * _dot(y_at.astype(BF16), wab_ref[...])
    h = x_ref[0] + gt1 * _dot(mix.astype(BF16), wo_ref[...])

    ms = jnp.mean(h * h, axis=-1, keepdims=True)
    u2 = (h * lax.rsqrt(ms + EPS)) * (g2_ref[...] * (1.0 + sc2)) + sh2
    up = jnp.maximum(_dot(u2.astype(BF16), wup_ref[...]), 0.0)
    ff = _dot((up * up).astype(BF16), wdn_ref[...])
    out_ref[0] = h + gt2 * ff


def _merge_mlp(x, conv, w, x0, outs, lses, gates, mod, hyena_d, norm2_g, w_hy_br, w_at_br, w_out, w_up, w_down,
               tm):
    bsz, seq, _ = x.shape
    tile = lambda w: pl.BlockSpec((1, tm, w), lambda b, i: (b, i, 0))
    n_attn = len(outs) + len(lses)
    return pl.pallas_call(
        _merge_kernel,
        name="merge_mlp",
        grid=(bsz, seq // tm),
        in_specs=[tile(D_MODEL)] + [tile(D_HYENA)] * 3 + [tile(PAIR_WIDTH)] * n_attn + [tile(2 * D_MODEL),
                  pl.BlockSpec((1, 1, 6 * D_MODEL), lambda b, i: (b, 0, 0)),
                  _const_spec((1, D_HYENA)),
                  _const_spec((1, D_MODEL)),
                  _const_spec((D_HYENA, D_MODEL)),
                  _const_spec((PAIRS_PER_GROUP * PAIR_WIDTH, D_MODEL)),
                  _const_spec((D_MODEL, D_MODEL)),
                  _const_spec((D_MODEL, D_FF)),
                  _const_spec((D_FF, D_MODEL))],
        out_specs=tile(D_MODEL),
        out_shape=jax.ShapeDtypeStruct(x.shape, F32),
        compiler_params=_cparams("parallel", "parallel"),
    )(x, conv, w, x0, *outs, *lses, gates, mod, hyena_d.reshape(1, -1), norm2_g.reshape(1, -1), w_hy_br, w_at_br,
      w_out, w_up, w_down)


ATTN_TQ = {1: 512, 4: 256, 16: 128}


def _long_conv(w, kspec, consts):
    a = _dft_stage1(w, consts, is_filter=False)
    qf = _dft_stage2(a, kspec, consts["g2p"], consts["g2ip"], 13)
    return _dft_stage3(qf, consts)


def _filter_spec(seq, consts, filt):
    taps, asum = _filter_taps(seq, *filt)
    af = _dft_stage1(taps[None], consts, is_filter=True)
    return _filter_spectrum(af, consts["g2"], asum, 5)


def _encoder_layer(x, mod, tb, p):
    bsz, seq, _ = x.shape
    consts = {k: jnp.asarray(v) for k, v in _dft_consts(seq).items()}
    mod3 = mod.reshape(bsz, 1, 6 * D_MODEL)
    w, x0, q, k, v, gates = _in_projection(x, mod3, p["norm1_g"], p["w_in"], p["conv_w"], p["conv_b"],
                                           p["q_norm_g"], p["k_norm_g"], tm=512)
    kspec = _filter_spec(seq, consts, p["filt"])
    conv = _long_conv(w, kspec, consts)
    outs, lses = [], []
    for gi, (_, dil) in enumerate(DILATED_GROUPS):
        o, lse = _dilated_group(q, k, v, tb, gi, dil, ATTN_TQ[dil])
        outs.extend(o)
        lses.extend(lse)
    return _merge_mlp(x, conv, w, x0, outs, lses, gates, mod3, p["hyena_d"], p["norm2_g"], p["w_hy_br"],
                      p["w_at_br"], p["w_out"], p["w_up"], p["w_down"], tm=256)


def kernel(x_prompt, x_sample, c_prompt, c_sample, rel_bias, ada_w, ada_b, norm1_g, w_in, conv_w, conv_b,
           filt_w1, filt_b1, filt_w2, filt_b2, filt_w3, filt_b3, filt_freq, filt_w_out, hyena_d, q_norm_g,
           k_norm_g, w_hy_br, w_at_br, w_out, norm2_g, w_up, w_down):
    depth = ada_w.shape[0]
    tb = _bias_tables(rel_bias)
    y_prompt, y_sample = x_prompt, x_sample
    nbp = c_prompt.shape[0]
    for l in range(depth):
        p = dict(norm1_g=norm1_g[l], w_in=w_in[l].astype(BF16), conv_w=conv_w[l], conv_b=conv_b[l],
                 filt=(filt_w1[l], filt_b1[l], filt_w2[l], filt_b2[l], filt_w3[l], filt_b3[l], filt_freq[l],
                       filt_w_out[l]),
                 hyena_d=hyena_d[l], q_norm_g=q_norm_g[l], k_norm_g=k_norm_g[l],
                 w_hy_br=w_hy_br[l].astype(BF16), w_at_br=w_at_br[l].astype(BF16), w_out=w_out[l].astype(BF16),
                 norm2_g=norm2_g[l], w_up=w_up[l].astype(BF16), w_down=w_down[l].astype(BF16))
        mod = _modulation(jnp.concatenate([c_prompt, c_sample], axis=0), ada_w[l], ada_b[l])
        y_prompt = _encoder_layer(y_prompt, mod[:nbp], tb, p)
        y_sample = _encoder_layer(y_sample, mod[nbp:], tb, p)
    return (y_prompt, y_sample)
```

```python
import functools
import math

import numpy as np
import jax
import jax.numpy as jnp
from jax import lax
from jax.experimental import pallas as pl
from jax.experimental.pallas import tpu as pltpu

F32 = jnp.float32
BF16 = jnp.bfloat16
HIGHEST = lax.Precision.HIGHEST

D_MODEL = 1024
EPS = 1e-6
HEAD_DIM = 64
N_HEADS = 12
DILATED_GROUPS = ((128, 1), (512, 4), (2048, 16))
HEADS_PER_GROUP = 4
ATTN_WIDTH = 768
PAIR_WIDTH = 2 * HEAD_DIM
PAIRS_PER_GROUP = HEADS_PER_GROUP // 2
RADIUS = 64
NUM_BUCKETS = 32
MAX_DISTANCE = 1024
NEG_INF = -1e30
D_HYENA = 768
FILTER_BANDS = 16
FILTER_HIDDEN = 64
DECAY_TARGET = 1e-2
FAST_DECAY_PCT = 0.3
SLOW_DECAY_PCT = 1.5
D_FF = 4096
IN_PROJ_WIDTH = 6656

SUBLANES = 8
DFT_N1 = 128
DFT_H1 = 64
DFT_KP = 65
DFT_ROWS = DFT_KP * SUBLANES
QBLK = 128
VMEM_LIMIT = 56 * 1024 * 1024


def _cparams(*sem):
    return pltpu.CompilerParams(dimension_semantics=sem, vmem_limit_bytes=VMEM_LIMIT)


def _const_spec(shape):
    nd = len(shape)
    return pl.BlockSpec(shape, lambda *_: (0,) * nd, pipeline_mode=pl.Buffered(1))


def _dot(a, b, precision=None):
    return jnp.dot(a, b, preferred_element_type=F32, precision=precision)


@functools.lru_cache(maxsize=None)
def _dft_consts(L):
    n2n = L // DFT_H1
    n = 2 * L
    nch = n2n // SUBLANES
    r = np.arange(SUBLANES)
    eye = np.eye(SUBLANES)
    k1 = np.arange(DFT_KP)
    n1 = np.arange(DFT_H1)
    ang = -2.0 * np.pi * (r[:, None, None] * k1[None, :, None] / n + k1[None, :, None] * n1[None, None, :] / DFT_N1)
    f1 = np.stack([np.cos(ang), np.sin(ang)], axis=1)
    f0 = np.einsum("rpkn,rs->pkrns", f1, eye).reshape(2 * DFT_ROWS, DFT_H1 * SUBLANES)
    angc = 2.0 * np.pi * (n1[None, :, None] * k1[None, None, :] / DFT_N1 + r[:, None, None] * k1[None, None, :] / n)
    mult = np.full(DFT_KP, 2.0)
    mult[0] = mult[-1] = 1.0
    mi = np.stack([mult * np.cos(angc), -mult * np.sin(angc)], axis=2) / n
    m0 = np.einsum("rnpk,rs->nrpks", mi, eye).reshape(DFT_H1 * SUBLANES, 2 * DFT_ROWS)
    theta = -2.0 * np.pi * SUBLANES * np.arange(nch)[:, None] * k1[None, :] / n
    rep = lambda a: np.repeat(a, SUBLANES, axis=1)[:, :, None].astype(np.float32)
    sign = np.repeat(np.where(k1 % 2 == 0, 1.0, -1.0), SUBLANES)[:, None].astype(np.float32)
    a2 = -2.0 * np.pi * np.arange(n2n)[:, None] * np.arange(n2n)[None, :] / n2n
    fr, fi = np.cos(a2), np.sin(a2)
    g2 = np.block([[fr, -fi], [fi, fr]])
    g2i = np.block([[fr, fi], [-fi, fr]])
    as_bf16 = lambda a: np.asarray(a, np.float32).astype(BF16)
    f0p = np.einsum("rpkn,rs->kprns", f1, eye).reshape(2 * DFT_ROWS, DFT_H1 * SUBLANES)
    m0p = np.einsum("rnpk,rs->nrkps", mi, eye).reshape(DFT_H1 * SUBLANES, 2 * DFT_ROWS)
    jj, pp, rr = np.meshgrid(np.arange(nch), np.arange(2), r, indexing="ij")
    perm = (pp * n2n + SUBLANES * jj + rr).reshape(-1)
    per_k1 = lambda a: a[:, :, None, None].astype(np.float32)
    return dict(f0=as_bf16(f0), m0p=as_bf16(m0p), f0p=as_bf16(f0p), rot_c=rep(np.cos(theta)),
                rot_s=rep(np.sin(theta)), rotp_c=per_k1(np.cos(theta)), rotp_s=per_k1(np.sin(theta)), sign=sign,
                g2=as_bf16(g2), g2p=as_bf16(g2[:, perm]), g2ip=as_bf16(g2i[perm, :]))


def _t5_bucket_np(rel):
    half = NUM_BUCKETS // 2
    max_exact = half // 2
    n = np.abs(rel)
    ret = np.where(rel > 0, half, 0)
    large = max_exact + (np.log(np.maximum(n, 1).astype(np.float32) / np.float32(max_exact))
                         / np.float32(math.log(MAX_DISTANCE / max_exact))
                         * np.float32(half - max_exact)).astype(np.int32)
    large = np.minimum(large, half - 1)
    return ret + np.where(n < max_exact, n, large)


@functools.lru_cache(maxsize=None)
def _bucket_tables():
    qi = np.arange(QBLK)[:, None]
    c = np.arange(2 * QBLK)[None, :]
    rel = c - RADIUS - qi
    tabs = []
    for _, dil in DILATED_GROUPS:
        b = _t5_bucket_np(rel * dil)
        tabs.append(np.where(np.abs(rel) <= RADIUS, b, -1))
    return np.stack(tabs).astype(np.int32)


@functools.lru_cache(maxsize=None)
def _small_consts():
    bands = np.linspace(1e-4, FILTER_BANDS - 1, FILTER_BANDS, dtype=np.float32)[None, :]
    deltas = np.abs(np.linspace(math.log(DECAY_TARGET) / SLOW_DECAY_PCT,
                                math.log(DECAY_TARGET) / FAST_DECAY_PCT, D_HYENA, dtype=np.float32))[None, :]
    head = np.arange(ATTN_WIDTH) // HEAD_DIM
    member = (head[:, None] == np.arange(PAIR_WIDTH)[None, :]).astype(np.float32)
    return bands, deltas, ((member / HEAD_DIM).astype(BF16), member.T.astype(BF16))


def _mod_kernel(c_ref, w_ref, b_ref, o_ref):
    c = c_ref[...]
    s = c / (1.0 + jnp.exp(-c))
    o_ref[...] = _dot(s, w_ref[...], HIGHEST) + b_ref[...]


def _modulation(c, ada_w, ada_b):
    nb, _ = c.shape
    nw = ada_w.shape[1]
    tn = 1024
    return pl.pallas_call(
        _mod_kernel,
        name="mod",
        grid=(nw // tn,),
        in_specs=[pl.BlockSpec((nb, D_MODEL), lambda j: (0, 0)),
                  pl.BlockSpec((D_MODEL, tn), lambda j: (0, j)),
                  pl.BlockSpec((1, tn), lambda j: (0, j))],
        out_specs=pl.BlockSpec((nb, tn), lambda j: (0, j)),
        out_shape=jax.ShapeDtypeStruct((nb, nw), F32),
        compiler_params=_cparams("parallel"),
    )(c, ada_w, ada_b.reshape(1, nw))


def _bias_kernel(rb_ref, bk_ref, o_ref):
    g = pl.program_id(0)
    bk = bk_ref[0]
    for h in range(HEADS_PER_GROUP):
        acc = jnp.full(bk.shape, NEG_INF, F32)
        for b in range(NUM_BUCKETS):
            acc = jnp.where(bk == b, rb_ref[b, g * HEADS_PER_GROUP + h], acc)
        o_ref[0, h] = acc


def _bias_tables(rel_bias):
    bk = jnp.asarray(_bucket_tables())
    ng = len(DILATED_GROUPS)
    return pl.pallas_call(
        _bias_kernel,
        name="bias_tab",
        grid=(ng,),
        in_specs=[pl.BlockSpec(memory_space=pltpu.SMEM),
                  pl.BlockSpec((1, QBLK, 2 * QBLK), lambda g: (g, 0, 0))],
        out_specs=pl.BlockSpec((1, HEADS_PER_GROUP, QBLK, 2 * QBLK), lambda g: (g, 0, 0, 0)),
        out_shape=jax.ShapeDtypeStruct((ng, HEADS_PER_GROUP, QBLK, 2 * QBLK), F32),
        compiler_params=_cparams("arbitrary"),
    )(rel_bias, bk)


def _filter_kernel(w1t_ref, w1c_ref, w1s_ref, b1_ref, w2_ref, b2_ref, w3_ref, b3_ref, fr_ref, wo_ref,
                   bands_ref, deltas_ref, k_ref, asum_ref, *, seq, rows):
    i = pl.program_id(0)

    def lag(shape, axis):
        m = i * rows + lax.broadcasted_iota(jnp.int32, shape, axis)
        return m, jnp.where(m < seq, m, 2 * seq - m).astype(F32)

    _, pos_l = lag((1, rows), 1)
    t_l = pos_l / float(seq - 1)
    arg = bands_ref[...] * ((2.0 * math.pi) * pos_l / float(seq))
    fr = fr_ref[...]
    z = w1t_ref[...] * t_l + _dot(w1c_ref[...], jnp.cos(arg), HIGHEST) \
        + _dot(w1s_ref[...], -jnp.sin(arg), HIGHEST) + b1_ref[...]
    h = jnp.sin(fr * z)
    h = jnp.sin(fr * (_dot(w2_ref[...], h, HIGHEST) + b2_ref[...]))
    h = jnp.sin(fr * (_dot(w3_ref[...], h, HIGHEST) + b3_ref[...]))
    m, pos = lag((rows, 1), 0)
    t = pos / float(seq - 1)
    k = _dot(h.T, wo_ref[...], HIGHEST) * jnp.exp(-t * deltas_ref[...])
    k = jnp.where(m == seq, 0.0, k)
    k_ref[...] = k

    @pl.when(i == 0)
    def _():
        asum_ref[...] = jnp.zeros_like(asum_ref)

    asum_ref[...] += jnp.sum(jnp.abs(k), axis=0, keepdims=True)


def _filter_taps(seq, w1, b1, w2, b2, w3, b3, freq, w_out):
    bands, deltas, _ = _small_consts()
    rows = 1024
    nblk = 2 * seq // rows
    half = nblk // 2
    hid = FILTER_HIDDEN
    col = lambda a: a.reshape(-1, 1)
    small = lambda shape: pl.BlockSpec(shape, lambda i: (0, 0))
    return pl.pallas_call(
        functools.partial(_filter_kernel, seq=seq, rows=rows),
        name="filter_taps",
        grid=(nblk,),
        in_specs=[small((hid, 1)), small((hid, FILTER_BANDS)), small((hid, FILTER_BANDS)), small((hid, 1)),
                  small((hid, hid)), small((hid, 1)), small((hid, hid)), small((hid, 1)), small((hid, 1)),
                  pl.BlockSpec((hid, D_HYENA), lambda i: (0, jnp.where(i >= half, 1, 0))),
                  small((FILTER_BANDS, 1)), small((1, D_HYENA))],
        out_specs=[pl.BlockSpec((rows, D_HYENA), lambda i: (i, 0)),
                   pl.BlockSpec((1, D_HYENA), lambda i: (0, 0))],
        out_shape=[jax.ShapeDtypeStruct((2 * seq, D_HYENA), F32),
                   jax.ShapeDtypeStruct((1, D_HYENA), F32)],
        compiler_params=_cparams("arbitrary"),
    )(w1[0:1].T, w1[1:1 + FILTER_BANDS].T, w1[1 + FILTER_BANDS:].T, col(b1), w2.T, col(b2), w3.T, col(b3),
      col(freq), w_out, jnp.asarray(bands.T), jnp.asarray(deltas))


def _rotate(re, im, c, s):
    return c * re - s * im, s * re + c * im


def _store_spectrum_block(o_ref, re, im):
    c = re.shape[-1]
    o_ref[0, 0] = re.reshape(DFT_KP, SUBLANES, c)
    o_ref[0, 1] = im.reshape(DFT_KP, SUBLANES, c)


def _dft1_kernel(x_ref, f0_ref, rc_ref, rs_ref, o_ref):
    c = x_ref.shape[-1]
    x = x_ref[0].reshape(DFT_H1 * SUBLANES, c).astype(BF16)
    a = _dot(f0_ref[...], x).reshape(DFT_KP, 2, SUBLANES, c)
    re, im = _rotate(a[:, 0], a[:, 1], rc_ref[0], rs_ref[0])
    o_ref[0, :, 0] = jnp.concatenate([re, im], axis=1).astype(BF16)


def _dft1_filter_kernel(lo_ref, hi_ref, f0_ref, rc_ref, rs_ref, sign_ref, o_ref):
    c = lo_ref.shape[-1]
    flat = lambda ref: ref[0].reshape(DFT_H1 * SUBLANES, c).astype(BF16)
    lo = _dot(f0_ref[...], flat(lo_ref))
    hi = _dot(f0_ref[...], flat(hi_ref))
    sign = sign_ref[...]
    re, im = _rotate(lo[:DFT_ROWS] + sign * hi[:DFT_ROWS], lo[DFT_ROWS:] + sign * hi[DFT_ROWS:],
                     rc_ref[0], rs_ref[0])
    _store_spectrum_block(o_ref, re, im)


def _dft_stage1(x, consts, is_filter):
    bsz, rows, c = x.shape
    n2n = rows // (DFT_N1 if is_filter else DFT_H1)
    nch = n2n // SUBLANES
    rot = pl.BlockSpec((1, DFT_ROWS, 1), lambda j, b: (j, 0, 0))
    f0 = _const_spec((2 * DFT_ROWS, DFT_H1 * SUBLANES))
    out_spec = pl.BlockSpec((1, 2, DFT_KP, SUBLANES, c), lambda j, b: (b, 0, 0, j, 0))
    out_shape = jax.ShapeDtypeStruct((bsz, 2, DFT_KP, n2n, c), F32)
    if is_filter:
        xv = x.reshape(2, DFT_H1, n2n, c)
        data = [pl.BlockSpec((1, DFT_H1, SUBLANES, c), lambda j, b: (0, 0, j, 0)),
                pl.BlockSpec((1, DFT_H1, SUBLANES, c), lambda j, b: (1, 0, j, 0))]
        args = (xv, xv, consts["f0"], consts["rot_c"], consts["rot_s"], consts["sign"])
        specs = data + [f0, rot, rot, _const_spec((DFT_ROWS, 1))]
        body = _dft1_filter_kernel
    else:
        xv = x.reshape(bsz, DFT_H1, n2n, c)
        rotp = pl.BlockSpec((1, DFT_KP, 1, 1), lambda j, b: (j, 0, 0, 0))
        args = (xv, consts["f0p"], consts["rotp_c"], consts["rotp_s"])
        specs = [pl.BlockSpec((1, DFT_H1, SUBLANES, c), lambda j, b: (b, 0, j, 0)), f0, rotp, rotp]
        body = _dft1_kernel
        out_spec = pl.BlockSpec((1, DFT_KP, 1, 2 * SUBLANES, c), lambda j, b: (b, 0, j, 0, 0))
        out_shape = jax.ShapeDtypeStruct((bsz, DFT_KP, nch, 2 * SUBLANES, c), BF16)
    return pl.pallas_call(
        body,
        name="dft1",
        grid=(nch, bsz),
        in_specs=specs,
        out_specs=out_spec,
        out_shape=out_shape,
        compiler_params=_cparams("parallel", "parallel"),
    )(*args)


def _filter_spec_kernel(a_ref, g2_ref, asum_ref, o_ref, *, kb, n2n):
    scale = 1.0 / asum_ref[...]
    for j in range(kb):
        a = jnp.concatenate([a_ref[0, 0, j], a_ref[0, 1, j]], axis=0).astype(BF16)
        b = _dot(g2_ref[...], a) * scale
        o_ref[0, j] = b[:n2n]
        o_ref[1, j] = b[n2n:]


def _filter_spectrum(af, g2, asum, kb):
    _, _, kp, n2n, c = af.shape
    return pl.pallas_call(
        functools.partial(_filter_spec_kernel, kb=kb, n2n=n2n),
        name="filter_spec",
        grid=(kp // kb,),
        in_specs=[pl.BlockSpec((1, 2, kb, n2n, c), lambda i: (0, 0, i, 0, 0)),
                  pl.BlockSpec((2 * n2n, 2 * n2n), lambda i: (0, 0)),
                  pl.BlockSpec((1, c), lambda i: (0, 0))],
        out_specs=pl.BlockSpec((2, kb, n2n, c), lambda i: (0, i, 0, 0)),
        out_shape=jax.ShapeDtypeStruct((2, kp, n2n, c), F32),
        compiler_params=_cparams("parallel"),
    )(af, g2, asum)


def _dft2_kernel(a_ref, ks_ref, g2_ref, g2i_ref, o_ref, *, kb, n2n):
    nch, rows, c = a_ref.shape[2:]
    for j in range(kb):
        a = a_ref[0, j].reshape(2 * n2n, c)
        b = _dot(g2_ref[...], a)
        br, bi = b[:n2n], b[n2n:]
        kr, ki = ks_ref[0, j], ks_ref[1, j]
        p = jnp.concatenate([br * kr - bi * ki, br * ki + bi * kr], axis=0).astype(BF16)
        q = _dot(g2i_ref[...], p)
        o_ref[0, j] = q.astype(BF16).reshape(nch, rows, c)


def _dft_stage2(a, kspec, g2p, g2ip, kb):
    bsz, kp, nch, rows, c = a.shape
    n2n = nch * SUBLANES
    blk = pl.BlockSpec((1, kb, nch, rows, c), lambda i, b: (b, i, 0, 0, 0))
    return pl.pallas_call(
        functools.partial(_dft2_kernel, kb=kb, n2n=n2n),
        name="dft2",
        grid=(kp // kb, bsz),
        in_specs=[blk,
                  pl.BlockSpec((2, kb, n2n, c), lambda i, b: (0, i, 0, 0)),
                  pl.BlockSpec((2 * n2n, 2 * n2n), lambda i, b: (0, 0)),
                  pl.BlockSpec((2 * n2n, 2 * n2n), lambda i, b: (0, 0))],
        out_specs=blk,
        out_shape=jax.ShapeDtypeStruct(a.shape, BF16),
        compiler_params=_cparams("parallel", "parallel"),
    )(a, kspec, g2p, g2ip)


def _dft3_kernel(q_ref, m0_ref, rc_ref, rs_ref, y_ref):
    c = y_ref.shape[-1]
    q = q_ref[0, :, 0].astype(F32).reshape(DFT_KP, 2, SUBLANES, c)
    re, im = _rotate(q[:, 0], q[:, 1], rc_ref[0], -rs_ref[0])
    q = jnp.concatenate([re, im], axis=1).reshape(2 * DFT_ROWS, c).astype(BF16)
    y_ref[0] = _dot(m0_ref[...], q).reshape(DFT_H1, SUBLANES, c)


def _dft_stage3(q, consts):
    bsz, _, nch, rows, c = q.shape
    n2n = nch * SUBLANES
    rot = pl.BlockSpec((1, DFT_KP, 1, 1), lambda j, b: (j, 0, 0, 0))
    y = pl.pallas_call(
        _dft3_kernel,
        name="dft3",
        grid=(nch, bsz),
        in_specs=[pl.BlockSpec((1, DFT_KP, 1, rows, c), lambda j, b: (b, 0, j, 0, 0)),
                  _const_spec((DFT_H1 * SUBLANES, 2 * DFT_ROWS)), rot, rot],
        out_specs=pl.BlockSpec((1, DFT_H1, SUBLANES, c), lambda j, b: (b, 0, j, 0)),
        out_shape=jax.ShapeDtypeStruct((bsz, DFT_H1, n2n, c), F32),
        compiler_params=_cparams("parallel", "parallel"),
    )(q, consts["m0p"], consts["rotp_c"], consts["rotp_s"])
    return y.reshape(bsz, DFT_H1 * n2n, c)


def _inproj_kernel(x_ref, xp_ref, xn_ref, mod_ref, g1_ref, win_ref, cw_ref, cb_ref, gq_ref, gk_ref, hr_ref, he_ref,
                   w_out, x0_out, q_out, k_out, v_out, g_out, *, tm, nt):
    i = pl.program_id(1)
    mod = mod_ref[0]
    sh1 = mod[:, 0:D_MODEL]
    scale1 = g1_ref[...] * (1.0 + mod[:, D_MODEL:2 * D_MODEL])

    def norm_mod(x):
        ms = jnp.mean(x * x, axis=-1, keepdims=True)
        return ((x * lax.rsqrt(ms + EPS)) * scale1 + sh1).astype(BF16)

    u = norm_mod(x_ref[0])
    uh = norm_mod(jnp.concatenate([xp_ref[0], xn_ref[0]], axis=0))
    ue = jnp.concatenate([u, uh], axis=0)
    row = lax.broadcasted_iota(jnp.int32, (tm, 1), 0)

    def conv_chunk(c):
        cols = slice(c * D_HYENA, (c + 1) * D_HYENA)
        ze = _dot(ue, win_ref[:, cols])
        z, zh = ze[:tm], ze[tm:]
        prev = jnp.where(i > 0, zh[7:8], 0.0)
        nxt = jnp.where(i < nt - 1, zh[8:9], 0.0)
        zm = jnp.where(row == 0, prev, pltpu.roll(z, 1, 0))
        zp = jnp.where(row == tm - 1, nxt, pltpu.roll(z, tm - 1, 0))
        return zm * cw_ref[0:1, cols] + z * cw_ref[1:2, cols] + zp * cw_ref[2:3, cols] + cb_ref[:, cols]

    x0_out[0] = conv_chunk(0)
    w_out[0] = conv_chunk(1) * conv_chunk(2)

    base = 3 * D_HYENA

    def head_norm(z, gain):
        per_head = _dot((z * z).astype(BF16), hr_ref[...])
        ms = _dot(per_head.astype(BF16), he_ref[...])
        return (z * lax.rsqrt(ms + EPS)) * gain

    zq = _dot(u, win_ref[:, base:base + ATTN_WIDTH])
    q_out[0] = head_norm(zq, gq_ref[...]) * (HEAD_DIM ** -0.5)
    zk = _dot(u, win_ref[:, base + ATTN_WIDTH:base + 2 * ATTN_WIDTH])
    k_out[0] = head_norm(zk, gk_ref[...])
    v_out[0] = _dot(u, win_ref[:, base + 2 * ATTN_WIDTH:base + 3 * ATTN_WIDTH])
    zg = _dot(u, win_ref[:, base + 3 * ATTN_WIDTH:])
    g_out[0] = 1.0 / (1.0 + jnp.exp(-zg))


def _in_projection(x, mod, norm1_g, w_in_bf16, conv_w, conv_b, q_norm_g, k_norm_g, tm):
    bsz, seq, _ = x.shape
    nt = seq // tm
    hb = tm // SUBLANES
    _, _, (head_reduce, head_expand) = _small_consts()
    tile = lambda w: pl.BlockSpec((1, tm, w), lambda b, i: (b, i, 0))
    outs = [D_HYENA, D_HYENA, ATTN_WIDTH, ATTN_WIDTH, ATTN_WIDTH, 2 * D_MODEL]
    return pl.pallas_call(
        functools.partial(_inproj_kernel, tm=tm, nt=nt),
        name="inproj",
        grid=(bsz, nt),
        in_specs=[tile(D_MODEL),
                  pl.BlockSpec((1, SUBLANES, D_MODEL), lambda b, i: (b, jnp.maximum(i * hb - 1, 0), 0)),
                  pl.BlockSpec((1, SUBLANES, D_MODEL),
                               lambda b, i: (b, jnp.minimum((i + 1) * hb, seq // SUBLANES - 1), 0)),
                  pl.BlockSpec((1, 1, 6 * D_MODEL), lambda b, i: (b, 0, 0)),
                  _const_spec((1, D_MODEL)),
                  _const_spec((D_MODEL, IN_PROJ_WIDTH)),
                  _const_spec((3, 3 * D_HYENA)),
                  _const_spec((1, 3 * D_HYENA)),
                  _const_spec((1, ATTN_WIDTH)),
                  _const_spec((1, ATTN_WIDTH)),
                  _const_spec((ATTN_WIDTH, PAIR_WIDTH)),
                  _const_spec((PAIR_WIDTH, ATTN_WIDTH))],
        out_specs=[tile(w) for w in outs],
        out_shape=[jax.ShapeDtypeStruct((bsz, seq, w), F32) for w in outs],
        compiler_params=_cparams("parallel", "parallel"),
    )(x, x, x, mod, norm1_g.reshape(1, -1), w_in_bf16, conv_w, conv_b.reshape(1, -1),
      q_norm_g.reshape(1, -1), k_norm_g.reshape(1, -1), jnp.asarray(head_reduce), jnp.asarray(head_expand))


def _attn_kernel(*refs, tq, sub, dil):
    npair = PAIRS_PER_GROUP
    q_refs = refs[0:npair]
    k_refs = refs[npair:4 * npair]
    v_refs = refs[4 * npair:7 * npair]
    tb_ref = refs[7 * npair]
    o_refs = refs[7 * npair + 1:8 * npair + 1]
    l_refs = refs[8 * npair + 1:9 * npair + 1]
    t0 = pl.program_id(1) * tq
    lower = lax.broadcasted_iota(jnp.int32, (1, PAIR_WIDTH), 1) < HEAD_DIM

    def rows(r, count, offset=0):
        if dil == 1:
            return pl.ds(offset, count)
        return pl.ds(r + offset * dil, count, stride=dil)

    def residue(r):
        for pair in range(npair):
            kp, kc, kn = k_refs[3 * pair:3 * pair + 3]
            vp, vc, vn = v_refs[3 * pair:3 * pair + 3]
            kfull = jnp.concatenate([kp[0, rows(r, RADIUS), :], kc[0, rows(r, tq), :], kn[0, rows(r, RADIUS), :]],
                                    axis=0).astype(BF16)
            vfull = jnp.concatenate([vp[0, rows(r, RADIUS), :], vc[0, rows(r, tq), :], vn[0, rows(r, RADIUS), :]],
                                    axis=0).astype(BF16)
            for j in range(tq // QBLK):
                kpos = t0 - RADIUS + QBLK * j + lax.broadcasted_iota(jnp.int32, (1, 2 * QBLK), 1)
                valid = (kpos >= 0) & (kpos < sub)
                win = slice(QBLK * j, QBLK * (j + 2))
                q2 = q_refs[pair][0, rows(r, QBLK, QBLK * j), :]
                k2 = kfull[win]
                v2 = vfull[win]
                res = []
                for hh in range(2):
                    sel = lower if hh == 0 else jnp.logical_not(lower)
                    qm = jnp.where(sel, q2, 0.0).astype(BF16)
                    s = lax.dot_general(qm, k2, (((1,), (1,)), ((), ())), preferred_element_type=F32)
                    s = jnp.where(valid, s + tb_ref[0, 2 * pair + hh], NEG_INF)
                    m = jnp.max(s, axis=-1, keepdims=True)
                    p = jnp.exp(s - m)
                    den = jnp.sum(p, axis=-1, keepdims=True)
                    res.append((_dot(p.astype(BF16), v2) / den, m + jnp.log(den)))
                o_refs[pair][0, rows(r, QBLK, QBLK * j), :] = jnp.where(lower, res[0][0], res[1][0])
                l_refs[pair][0, rows(r, QBLK, QBLK * j), :] = jnp.where(lower, res[0][1], res[1][1])

    if dil == 1:
        residue(0)
    else:
        def body(r, carry):
            residue(r)
            return carry
        lax.fori_loop(0, dil, body, 0, unroll=4)


def _dilated_group(q, k, v, tb, gi, dil, tq):
    bsz, seq, _ = q.shape
    sub = seq // dil
    tq = min(tq, sub)
    tp = tq * dil
    halo = RADIUS * dil
    hb = tp // halo
    lane0 = gi * PAIRS_PER_GROUP
    cur = lambda p: pl.BlockSpec((1, tp, PAIR_WIDTH), lambda b, i: (b, i, lane0 + p))
    prev = lambda p: pl.BlockSpec((1, halo, PAIR_WIDTH), lambda b, i: (b, jnp.maximum(i * hb - 1, 0), lane0 + p))
    nxt = lambda p: pl.BlockSpec((1, halo, PAIR_WIDTH),
                                 lambda b, i: (b, jnp.minimum((i + 1) * hb, seq // halo - 1), lane0 + p))
    out = pl.BlockSpec((1, tp, PAIR_WIDTH), lambda b, i: (b, i, 0))
    pairs = range(PAIRS_PER_GROUP)
    halo_specs = [s(p) for p in pairs for s in (prev, cur, nxt)]
    res = pl.pallas_call(
        functools.partial(_attn_kernel, tq=tq, sub=sub, dil=dil),
        name="attn",
        grid=(bsz, seq // tp),
        in_specs=[cur(p) for p in pairs] + halo_specs + halo_specs
                 + [pl.BlockSpec((1, HEADS_PER_GROUP, QBLK, 2 * QBLK), lambda b, i: (gi, 0, 0, 0))],
        out_specs=[out] * (2 * PAIRS_PER_GROUP),
        out_shape=[jax.ShapeDtypeStruct((bsz, seq, PAIR_WIDTH), F32)] * (2 * PAIRS_PER_GROUP),
        compiler_params=_cparams("parallel", "parallel"),
    )(*([q] * PAIRS_PER_GROUP + [k] * (3 * PAIRS_PER_GROUP) + [v] * (3 * PAIRS_PER_GROUP) + [tb]))
    return res[:PAIRS_PER_GROUP], res[PAIRS_PER_GROUP:]


def _merge_kernel(*refs):
    ng = len(DILATED_GROUPS)
    npair = PAIRS_PER_GROUP
    x_ref, conv_ref, w_ref, x0_ref = refs[0:4]
    o_refs = refs[4:4 + ng * npair]
    l_refs = refs[4 + ng * npair:4 + 2 * ng * npair]
    g_ref, mod_ref, d_ref, g2_ref, whb_ref, wab_ref, wo_ref, wup_ref, wdn_ref, out_ref = refs[4 + 2 * ng * npair:]
    mod = mod_ref[0]
    gt1 = mod[:, 2 * D_MODEL:3 * D_MODEL]
    sh2 = mod[:, 3 * D_MODEL:4 * D_MODEL]
    sc2 = mod[:, 4 * D_MODEL:5 * D_MODEL]
    gt2 = mod[:, 5 * D_MODEL:6 * D_MODEL]

    merged = []
    for p in range(npair):
        lses = [l_refs[g * npair + p][0] for g in range(ng)]
        lm = functools.reduce(jnp.maximum, lses)
        es = [jnp.exp(l - lm) for l in lses]
        num = sum(e * o_refs[g * npair + p][0] for g, e in enumerate(es))
        merged.append(num / sum(es))
    y_at = jnp.concatenate(merged, axis=-1)

    y_hy = x0_ref[0] * (conv_ref[0] + w_ref[0] * d_ref[...])
    g = g_ref[0]
    mix = g[:, :D_MODEL] * _dot(y_hy.astype(BF16), whb_ref[...]) \
        + g[:, D_MODEL:] * _dot(y_at.astype(BF16), wab_ref[...])
    h = x_ref[0] + gt1 * _dot(mix.astype(BF16), wo_ref[...])

    ms = jnp.mean(h * h, axis=-1, keepdims=True)
    u2 = (h * lax.rsqrt(ms + EPS)) * (g2_ref[...] * (1.0 + sc2)) + sh2
    up = jnp.maximum(_dot(u2.astype(BF16), wup_ref[...]), 0.0)
    ff = _dot((up * up).astype(BF16), wdn_ref[...])
    out_ref[0] = h + gt2 * ff


def _merge_mlp(x, conv, w, x0, outs, lses, gates, mod, hyena_d, norm2_g, w_hy_br, w_at_br, w_out, w_up, w_down,
               tm):
    bsz, seq, _ = x.shape
    tile = lambda w: pl.BlockSpec((1, tm, w), lambda b, i: (b, i, 0))
    n_attn = len(outs) + len(lses)
    return pl.pallas_call(
        _merge_kernel,
        name="merge_mlp",
        grid=(bsz, seq // tm),
        in_specs=[tile(D_MODEL)] + [tile(D_HYENA)] * 3 + [tile(PAIR_WIDTH)] * n_attn + [tile(2 * D_MODEL),
                  pl.BlockSpec((1, 1, 6 * D_MODEL), lambda b, i: (b, 0, 0)),
                  _const_spec((1, D_HYENA)),
                  _const_spec((1, D_MODEL)),
                  _const_spec((D_HYENA, D_MODEL)),
                  _const_spec((PAIRS_PER_GROUP * PAIR_WIDTH, D_MODEL)),
                  _const_spec((D_MODEL, D_MODEL)),
                  _const_spec((D_MODEL, D_FF)),
                  _const_spec((D_FF, D_MODEL))],
        out_specs=tile(D_MODEL),
        out_shape=jax.ShapeDtypeStruct(x.shape, F32),
        compiler_params=_cparams("parallel", "parallel"),
    )(x, conv, w, x0, *outs, *lses, gates, mod, hyena_d.reshape(1, -1), norm2_g.reshape(1, -1), w_hy_br, w_at_br,
      w_out, w_up, w_down)


ATTN_TQ = {1: 512, 4: 256, 16: 128}


def _long_conv(w, kspec, consts):
    a = _dft_stage1(w, consts, is_filter=False)
    qf = _dft_stage2(a, kspec, consts["g2p"], consts["g2ip"], 13)
    return _dft_stage3(qf, consts)


def _filter_spec(seq, consts, filt):
    taps, asum = _filter_taps(seq, *filt)
    af = _dft_stage1(taps[None], consts, is_filter=True)
    return _filter_spectrum(af, consts["g2"], asum, 5)


def _encoder_layer(x, mod, tb, p):
    bsz, seq, _ = x.shape
    consts = {k: jnp.asarray(v) for k, v in _dft_consts(seq).items()}
    mod3 = mod.reshape(bsz, 1, 6 * D_MODEL)
    w, x0, q, k, v, gates = _in_projection(x, mod3, p["norm1_g"], p["w_in"], p["conv_w"], p["conv_b"],
                                           p["q_norm_g"], p["k_norm_g"], tm=512)
    kspec = _filter_spec(seq, consts, p["filt"])
    conv = _long_conv(w, kspec, consts)
    outs, lses = [], []
    for gi, (_, dil) in enumerate(DILATED_GROUPS):
        o, lse = _dilated_group(q, k, v, tb, gi, dil, ATTN_TQ[dil])
        outs.extend(o)
        lses.extend(lse)
    return _merge_mlp(x, conv, w, x0, outs, lses, gates, mod3, p["hyena_d"], p["norm2_g"], p["w_hy_br"],
                      p["w_at_br"], p["w_out"], p["w_up"], p["w_down"], tm=256)


def kernel(x_prompt, x_sample, c_prompt, c_sample, rel_bias, ada_w, ada_b, norm1_g, w_in, conv_w, conv_b,
           filt_w1, filt_b1, filt_w2, filt_b2, filt_w3, filt_b3, filt_freq, filt_w_out, hyena_d, q_norm_g,
           k_norm_g, w_hy_br, w_at_br, w_out, norm2_g, w_up, w_down):
    depth = ada_w.shape[0]
    tb = _bias_tables(rel_bias)
    y_prompt, y_sample = x_prompt, x_sample
    nbp = c_prompt.shape[0]
    for l in range(depth):
        p = dict(norm1_g=norm1_g[l], w_in=w_in[l].astype(BF16), conv_w=conv_w[l], conv_b=conv_b[l],
                 filt=(filt_w1[l], filt_b1[l], filt_w2[l], filt_b2[l], filt_w3[l], filt_b3[l], filt_freq[l],
                       filt_w_out[l]),
                 hyena_d=hyena_d[l], q_norm_g=q_norm_g[l], k_norm_g=k_norm_g[l],
                 w_hy_br=w_hy_br[l].astype(BF16), w_at_br=w_at_br[l].astype(BF16), w_out=w_out[l].astype(BF16),
                 norm2_g=norm2_g[l], w_up=w_up[l].astype(BF16), w_down=w_down[l].astype(BF16))
        mod = _modulation(jnp.concatenate([c_prompt, c_sample], axis=0), ada_w[l], ada_b[l])
        y_prompt = _encoder_layer(y_prompt, mod[:nbp], tb, p)
        y_sample = _encoder_layer(y_sample, mod[nbp:], tb, p)
    return (y_prompt, y_sample)
```
